```python
import jax, jax.numpy as jnp
from jax import lax
import numpy as np

D_MODEL = 1024
BATCH = 8
SEQ = 4096
DEPTH = 2

HEAD_DIM = 64
N_Q_HEADS = 8
N_KV_HEADS = 2
GQA_GROUP = N_Q_HEADS // N_KV_HEADS
WINDOW = 128
ROPE_THETA = 10000.0
RET_HEADS = 4
RET_QK_DIM = 128
RET_V_DIM = 2 * RET_QK_DIM
RET_CHUNK = 128
RET_THETA = 10000.0
D_FF = 4 * D_MODEL
EPS = 1e-6

ATT_Q = N_Q_HEADS * HEAD_DIM
ATT_KV = N_KV_HEADS * HEAD_DIM
RET_QK = RET_HEADS * RET_QK_DIM
RET_V = RET_HEADS * RET_V_DIM
SPLIT_SIZES = (ATT_Q, ATT_KV, ATT_KV, RET_QK, RET_QK, RET_V, RET_V, D_MODEL, D_MODEL)
W_IN = sum(SPLIT_SIZES)
SPLIT_POINTS = tuple(int(v) for v in np.cumsum(SPLIT_SIZES)[:-1])

kernel_name = "hybrid_swa_sink_retention_gated"


def rmsnorm(x, g):
    xf = x.astype(jnp.float32)
    y = xf * lax.rsqrt(jnp.mean(xf * xf, axis=-1, keepdims=True) + EPS)
    return (y * g.astype(jnp.float32)).astype(x.dtype)


def rope_half(x, pos):
    half = x.shape[-1] // 2
    inv = ROPE_THETA ** (-jnp.arange(half, dtype=jnp.float32) / half)
    ang = pos.astype(jnp.float32)[:, None] * inv[None, :]
    cos = jnp.cos(ang)[None, :, None, :]
    sin = jnp.sin(ang)[None, :, None, :]
    xf = x.astype(jnp.float32)
    x1, x2 = xf[..., :half], xf[..., half:]
    return jnp.concatenate([x1 * cos - x2 * sin, x2 * cos + x1 * sin], axis=-1).astype(x.dtype)


def rotate_every_two(x):
    x1, x2 = x[..., ::2], x[..., 1::2]
    return jnp.stack([-x2, x1], axis=-1).reshape(x.shape)


def retention_rotation(x, pos):
    dk = x.shape[-1]
    theta = 1.0 / (RET_THETA ** jnp.linspace(0.0, 1.0, dk // 2, dtype=jnp.float32))
    theta = jnp.repeat(theta, 2)
    ang = pos.astype(jnp.float32)[:, None] * theta[None, :]
    cos = jnp.cos(ang)[None, :, None, :]
    sin = jnp.sin(ang)[None, :, None, :]
    xf = x.astype(jnp.float32)
    return (xf * cos + rotate_every_two(xf) * sin).astype(x.dtype)


def sliding_window_sink_attention(q, k, v, sinks):
    B, S = q.shape[0], q.shape[1]
    C = WINDOW
    nb = S // C
    qb = q.reshape(B, nb, C, N_KV_HEADS, GQA_GROUP, HEAD_DIM)
    kb = k.reshape(B, nb, C, N_KV_HEADS, HEAD_DIM)
    vb = v.reshape(B, nb, C, N_KV_HEADS, HEAD_DIM)

    def band(t):
        prev = jnp.pad(t, ((0, 0), (1, 0), (0, 0), (0, 0), (0, 0)))[:, :-1]
        return jnp.concatenate([prev, t], axis=2)

    kk, vv = band(kb), band(vb)
    s = jnp.einsum('bnqhgd,bnkhd->bnhgqk', qb, kk).astype(jnp.float32) * (HEAD_DIM ** -0.5)
    qi = jnp.arange(C)[:, None] + C
    kj = jnp.arange(2 * C)[None, :]
    rel = qi - kj
    valid = (rel >= 0) & (rel < WINDOW)
    first = (jnp.arange(nb) > 0)[:, None, None] | (kj >= C)[None]
    mask = valid[None] & first
    s = jnp.where(mask[None, :, None, None], s, -jnp.inf)
    sink = jnp.broadcast_to(
        sinks.astype(jnp.float32).reshape(1, 1, N_KV_HEADS, GQA_GROUP, 1, 1),
        s.shape[:-1] + (1,))
    p = jax.nn.softmax(jnp.concatenate([s, sink], axis=-1), axis=-1)[..., :-1]
    o = jnp.einsum('bnhgqk,bnkhd->bnqhgd', p.astype(v.dtype), vv)
    return o.reshape(B, S, ATT_Q)


def chunkwise_retention(q, k, v):
    B, S = q.shape[0], q.shape[1]
    C = RET_CHUNK
    nc = S // C
    log_g = jnp.log(1.0 - 2.0 ** (-5.0 - jnp.arange(RET_HEADS, dtype=jnp.float32)))
    idx = jnp.arange(C, dtype=jnp.float32)
    rel = idx[:, None] - idx[None, :]
    dmask = jnp.where(rel[None] >= 0, jnp.exp(log_g[:, None, None] * jnp.maximum(rel, 0.0)[None]), 0.0)

    qc = q.astype(jnp.float32).reshape(B, nc, C, RET_HEADS, RET_QK_DIM)
    kc = k.astype(jnp.float32).reshape(B, nc, C, RET_HEADS, RET_QK_DIM)
    vc = v.astype(jnp.float32).reshape(B, nc, C, RET_HEADS, RET_V_DIM)

    s = jnp.einsum('bnihd,bnjhd->bnhij', qc, kc) * dmask[None, None]
    inner = jnp.einsum('bnhij,bnjhe->bnihe', s, vc)

    zeta = jnp.exp(log_g[None, :] * (C - 1.0 - idx)[:, None])
    kv = jnp.einsum('bnjhd,bnjhe->nbhde', kc, vc * zeta[None, None, :, :, None])
    g_chunk = jnp.exp(log_g * C)[None, :, None, None]

    def step(R, kv_n):
        return R * g_chunk + kv_n, R

    R0 = jnp.zeros((B, RET_HEADS, RET_QK_DIM, RET_V_DIM), jnp.float32)
    _, R_prev = lax.scan(step, R0, kv)
    xi = jnp.exp(log_g[None, :] * (idx + 1.0)[:, None])
    cross = jnp.einsum('bnihd,nbhde->bnihe', qc, R_prev) * xi[None, None, :, :, None]
    return (inner + cross).reshape(B, S, RET_HEADS, RET_V_DIM)


def hybrid_layer(x, g_mix, w_in, sinks, w_a, w_b, w_out, g_mlp, w_up, w_down):
    B, S, _ = x.shape
    pos = jnp.arange(S)
    h = rmsnorm(x, g_mix)
    z = h @ w_in
    aq, ak, av, rq, rk, rv, rg, ga, gb = jnp.split(z, SPLIT_POINTS, axis=-1)

    aq = rope_half(aq.reshape(B, S, N_Q_HEADS, HEAD_DIM), pos)
    ak = rope_half(ak.reshape(B, S, N_KV_HEADS, HEAD_DIM), pos)
    av = av.reshape(B, S, N_KV_HEADS, HEAD_DIM)
    ya = sliding_window_sink_attention(aq, ak, av, sinks) @ w_a

    rq = retention_rotation(rq.reshape(B, S, RET_HEADS, RET_QK_DIM), pos)
    rk = retention_rotation(rk.reshape(B, S, RET_HEADS, RET_QK_DIM), pos) * (RET_QK_DIM ** -0.5)
    rv = rv.reshape(B, S, RET_HEADS, RET_V_DIM)
    r = chunkwise_retention(rq, rk, rv)
    r = r * lax.rsqrt(jnp.mean(r * r, axis=-1, keepdims=True) + EPS)
    r = r.reshape(B, S, RET_V) * jax.nn.silu(rg.astype(jnp.float32))
    yb = r.astype(x.dtype) @ w_b

    mixed = jax.nn.sigmoid(ga) * ya + jax.nn.sigmoid(gb) * yb
    x = x + mixed @ w_out

    h2 = rmsnorm(x, g_mlp)
    x = x + jnp.square(jax.nn.relu(h2 @ w_up)) @ w_down
    return x


def setup_inputs(seed: int = 0) -> dict:
    key = jax.random.key(seed)
    ks = jax.random.split(key, 12)
    f32 = jnp.float32

    def nrm(k, shape, fan_in):
        return jax.random.normal(k, shape, f32) * (fan_in ** -0.5)

    return {
        "x": jax.random.normal(ks[0], (BATCH, SEQ, D_MODEL), f32),
        "g_mix": 1.0 + 0.05 * jax.random.normal(ks[1], (DEPTH, D_MODEL), f32),
        "w_in": nrm(ks[2], (DEPTH, D_MODEL, W_IN), D_MODEL),
        "sinks": 0.5 * jax.random.normal(ks[3], (DEPTH, N_Q_HEADS), f32),
        "w_a": nrm(ks[4], (DEPTH, ATT_Q, D_MODEL), ATT_Q),
        "w_b": nrm(ks[5], (DEPTH, RET_V, D_MODEL), RET_V),
        "w_out": nrm(ks[6], (DEPTH, D_MODEL, D_MODEL), D_MODEL),
        "g_mlp": 1.0 + 0.05 * jax.random.normal(ks[7], (DEPTH, D_MODEL), f32),
        "w_up": nrm(ks[8], (DEPTH, D_MODEL, D_FF), D_MODEL),
        "w_down": nrm(ks[9], (DEPTH, D_FF, D_MODEL), D_FF),
        "g_final": 1.0 + 0.05 * jax.random.normal(ks[10], (D_MODEL,), f32),
    }


def reference(x, g_mix, w_in, sinks, w_a, w_b, w_out, g_mlp, w_up, w_down, g_final):
    for l in range(DEPTH):
        x = hybrid_layer(x, g_mix[l], w_in[l], sinks[l], w_a[l], w_b[l], w_out[l],
                         g_mlp[l], w_up[l], w_down[l])
    return rmsnorm(x, g_final)
```

```python
import functools

import jax
import jax.numpy as jnp
from jax import lax
from jax.experimental import pallas as pl
from jax.experimental.pallas import tpu as pltpu

D_MODEL = 1024
HEAD_DIM = 64
N_Q_HEADS = 8
N_KV_HEADS = 2
WINDOW = 128
ROPE_THETA = 10000.0
RET_HEADS = 4
RET_QK_DIM = 128
RET_V_DIM = 2 * RET_QK_DIM
RET_CHUNK = 128
RET_THETA = 10000.0
D_FF = 4 * D_MODEL
EPS = 1e-6

ATT_Q = N_Q_HEADS * HEAD_DIM
ATT_KV = N_KV_HEADS * HEAD_DIM
RET_QK = RET_HEADS * RET_QK_DIM
RET_V = RET_HEADS * RET_V_DIM
W_IN = ATT_Q + 2 * ATT_KV + 2 * RET_QK + 2 * RET_V + 2 * D_MODEL

OFF_AQ = 0
OFF_AK = OFF_AQ + ATT_Q
OFF_AV = OFF_AK + ATT_KV
OFF_RQ = OFF_AV + ATT_KV
OFF_RK = OFF_RQ + RET_QK
OFF_RV = OFF_RK + RET_QK
OFF_RG = OFF_RV + RET_V
OFF_GA = OFF_RG + RET_V
OFF_GB = OFF_GA + D_MODEL

LANES = 128
CHUNK = 128
Q_GROUPS = ATT_Q // LANES
SEQ_TILE = 512
MLP_TILE = 512
FF_CHUNK = 1024
VMEM_LIMIT_BYTES = 56 * 1024 * 1024

BF16 = jnp.bfloat16
F32 = jnp.float32


def _dot(a, b):
    return jnp.dot(a, b, preferred_element_type=F32)


def _dot_nt(a, b):
    return lax.dot_general(a, b, (((1,), (1,)), ((), ())), preferred_element_type=F32)


def _dot_tn(a, b):
    return lax.dot_general(a, b, (((0,), (0,)), ((), ())), preferred_element_type=F32)


def _sigmoid(x):
    return 1.0 / (1.0 + jnp.exp(-x))


def _rmsnorm(x, g):
    return x * lax.rsqrt(jnp.mean(x * x, axis=-1, keepdims=True) + EPS) * g


def _pair_swap(x, dist):
    lane = lax.broadcasted_iota(jnp.int32, x.shape, 1)
    first = (lane % (2 * dist)) < dist
    return jnp.where(first, pltpu.roll(x, LANES - dist, 1), pltpu.roll(x, dist, 1))


def _mixer_kernel(x_ref, g_ref, win_ref, wa_ref, wb_ref, wout_ref, sinks_ref, gch_ref,
                  cosa_ref, sina_ref, cosr_ref, sinr_ref, dmask_ref, zeta_ref, xi_ref,
                  o_ref,
                  h_scr, qlo_scr, qhi_scr, ka_scr, kb_scr, va_scr, vb_scr, oatt_scr,
                  rq_scr, rk_scr, rv_scr, r_scr, state_scr, mixed_scr):
    tm = x_ref.shape[1]
    nch = tm // CHUNK
    t = pl.program_id(1)

    @pl.when(t == 0)
    def _():
        zero_blk = jnp.zeros((CHUNK, LANES), BF16)
        ka_scr[0:CHUNK, :] = zero_blk
        kb_scr[0:CHUNK, :] = zero_blk
        va_scr[0:CHUNK, :] = zero_blk
        vb_scr[0:CHUNK, :] = zero_blk
        state_scr[...] = jnp.zeros(state_scr.shape, F32)

    x = x_ref[0]
    h_scr[...] = _rmsnorm(x, g_ref[...]).astype(BF16)
    h = h_scr[...]

    zatt = _dot(h, win_ref[:, OFF_AQ:OFF_RQ])
    cosa = cosa_ref[...]
    sina = sina_ref[...]
    lane = lax.broadcasted_iota(jnp.int32, (tm, LANES), 1)
    lo = lane < HEAD_DIM
    scale = HEAD_DIM ** -0.5
    for g in range(Q_GROUPS):
        zq = zatt[:, g * LANES:(g + 1) * LANES]
        q = (zq * cosa + _pair_swap(zq, HEAD_DIM // 2) * sina) * scale
        qlo_scr[:, g * LANES:(g + 1) * LANES] = jnp.where(lo, q, 0.0).astype(BF16)
        qhi_scr[:, g * LANES:(g + 1) * LANES] = jnp.where(lo, 0.0, q).astype(BF16)
    zk = zatt[:, OFF_AK:OFF_AV]
    k = zk * cosa + _pair_swap(zk, HEAD_DIM // 2) * sina
    v = zatt[:, OFF_AV:OFF_RQ]
    ka_scr[CHUNK:CHUNK + tm, :] = k.astype(BF16)
    kb_scr[CHUNK:CHUNK + tm, :] = pltpu.roll(k, HEAD_DIM, 1).astype(BF16)
    va_scr[CHUNK:CHUNK + tm, :] = v.astype(BF16)
    vb_scr[CHUNK:CHUNK + tm, :] = pltpu.roll(v, HEAD_DIM, 1).astype(BF16)

    rows = 4 * CHUNK
    qi = lax.broadcasted_iota(jnp.int32, (rows, 2 * CHUNK), 0) % CHUNK
    kj = lax.broadcasted_iota(jnp.int32, (rows, 2 * CHUNK), 1)
    band = (kj > qi) & (kj <= qi + WINDOW)
    band_first = band & ((kj >= CHUNK) | (t > 0))
    lo_c = lax.broadcasted_iota(jnp.int32, (CHUNK, LANES), 1) < HEAD_DIM

    def sink_col(heads):
        return jnp.concatenate([jnp.full((CHUNK, 1), sinks_ref[hd], F32) for hd in heads], axis=0)

    sink_a = sink_col((0, 2, 5, 7))
    sink_b = sink_col((1, 3, 4, 6))

    def attend(qs, kband, vband, mask, sink):
        s = jnp.where(mask, _dot_nt(qs, kband), -jnp.inf)
        m = jnp.maximum(jnp.max(s, axis=-1, keepdims=True), sink)
        p = jnp.exp(s - m)
        denom = jnp.sum(p, axis=-1, keepdims=True) + jnp.exp(sink - m)
        return _dot(p.astype(BF16), vband) / denom

    for c in range(nch):
        r0 = c * CHUNK
        mask = band_first if c == 0 else band
        ql = qlo_scr[r0:r0 + CHUNK, :]
        qh = qhi_scr[r0:r0 + CHUNK, :]
        qa = jnp.concatenate([ql[:, 0:LANES], ql[:, LANES:2 * LANES],
                              qh[:, 2 * LANES:3 * LANES], qh[:, 3 * LANES:4 * LANES]], axis=0)
        qb = jnp.concatenate([qh[:, 0:LANES], qh[:, LANES:2 * LANES],
                              ql[:, 2 * LANES:3 * LANES], ql[:, 3 * LANES:4 * LANES]], axis=0)
        oa = attend(qa, ka_scr[r0:r0 + 2 * CHUNK, :], va_scr[r0:r0 + 2 * CHUNK, :], mask, sink_a)
        ob = attend(qb, kb_scr[r0:r0 + 2 * CHUNK, :], vb_scr[r0:r0 + 2 * CHUNK, :], mask, sink_b)
        for g in range(Q_GROUPS):
            a_blk = oa[g * CHUNK:(g + 1) * CHUNK]
            b_blk = ob[g * CHUNK:(g + 1) * CHUNK]
            merged = jnp.where(lo_c, a_blk, b_blk) if g < 2 else jnp.where(lo_c, b_blk, a_blk)
            oatt_scr[r0:r0 + CHUNK, g * LANES:(g + 1) * LANES] = merged.astype(BF16)

    ka_scr[0:CHUNK, :] = ka_scr[tm:tm + CHUNK, :]
    kb_scr[0:CHUNK, :] = kb_scr[tm:tm + CHUNK, :]
    va_scr[0:CHUNK, :] = va_scr[tm:tm + CHUNK, :]
    vb_scr[0:CHUNK, :] = vb_scr[tm:tm + CHUNK, :]

    ya = _dot(oatt_scr[...], wa_ref[...])
    ga = _dot(h, win_ref[:, OFF_GA:OFF_GB])
    mixed_scr[...] = _sigmoid(ga) * ya

    zret = _dot(h, win_ref[:, OFF_RQ:OFF_RG])
    cosr = cosr_ref[...]
    sinr = sinr_ref[...]
    kscale = RET_QK_DIM ** -0.5
    for hd in range(RET_HEADS):
        zq = zret[:, hd * LANES:(hd + 1) * LANES]
        rq_scr[:, hd * LANES:(hd + 1) * LANES] = (zq * cosr + _pair_swap(zq, 1) * sinr).astype(BF16)
        zk = zret[:, RET_QK + hd * LANES:RET_QK + (hd + 1) * LANES]
        rk_scr[:, hd * LANES:(hd + 1) * LANES] = (
            (zk * cosr + _pair_swap(zk, 1) * sinr) * kscale).astype(BF16)
    rv_scr[...] = zret[:, 2 * RET_QK:2 * RET_QK + RET_V]

    for c in range(nch):
        r0 = c * CHUNK
        for hd in range(RET_HEADS):
            q = rq_scr[r0:r0 + CHUNK, hd * LANES:(hd + 1) * LANES]
            kk = rk_scr[r0:r0 + CHUNK, hd * LANES:(hd + 1) * LANES]
            vv = rv_scr[r0:r0 + CHUNK, hd * RET_V_DIM:(hd + 1) * RET_V_DIM]
            zeta = jnp.concatenate([zeta_ref[hd], zeta_ref[hd]], axis=1)
            xi = jnp.concatenate([xi_ref[hd], xi_ref[hd]], axis=1)
            state = state_scr[hd]
            s = _dot_nt(q, kk) * dmask_ref[hd]
            inner = _dot(s.astype(BF16), vv.astype(BF16))
            cross = _dot(q, state.astype(BF16)) * xi
            out = inner + cross
            kv = _dot_tn(kk, (vv * zeta).astype(BF16))
            state_scr[hd] = state * gch_ref[hd] + kv
            r_scr[r0:r0 + CHUNK, hd * RET_V_DIM:(hd + 1) * RET_V_DIM] = (
                out * lax.rsqrt(jnp.mean(out * out, axis=-1, keepdims=True) + EPS))

    rg = _dot(h, win_ref[:, OFF_RG:OFF_GA])
    rb = (r_scr[...] * (rg * _sigmoid(rg))).astype(BF16)
    yb = _dot(rb, wb_ref[...])
    gb = _dot(h, win_ref[:, OFF_GB:W_IN])
    mixed = mixed_scr[...] + _sigmoid(gb) * yb
    o_ref[0] = x_ref[0] + _dot(mixed.astype(BF16), wout_ref[...])


def _mlp_kernel(x_ref, g_ref, wup_ref, wdown_ref, gfin_ref, o_ref, *, final_norm):
    x = x_ref[...]
    h = _rmsnorm(x, g_ref[...]).astype(BF16)
    acc = jnp.zeros(x.shape, F32)
    for j in range(D_FF // FF_CHUNK):
        u = _dot(h, wup_ref[:, j * FF_CHUNK:(j + 1) * FF_CHUNK])
        a = jnp.square(jnp.maximum(u, 0.0)).astype(BF16)
        acc = acc + _dot(a, wdown_ref[j * FF_CHUNK:(j + 1) * FF_CHUNK, :])
    y = x + acc
    if final_norm:
        y = _rmsnorm(y, gfin_ref[...])
    o_ref[...] = y


def _resident(shape):
    return pl.BlockSpec(shape, lambda *_: (0,) * len(shape), pipeline_mode=pl.Buffered(1))


def _mixer_call(x, g, win, wa, wb, wout, sinks, gch, tables):
    bsz, seq, d = x.shape
    tm = min(SEQ_TILE, seq)
    cosa, sina, cosr, sinr, dmask, zeta, xi = tables
    pos_spec = pl.BlockSpec((tm, LANES), lambda b, t: (t, 0))
    smem = pl.BlockSpec(memory_space=pltpu.SMEM)
    x_spec = pl.BlockSpec((1, tm, d), lambda b, t: (b, t, 0))
    return pl.pallas_call(
        _mixer_kernel,
        grid=(bsz, seq // tm),
        in_specs=[x_spec, _resident((1, d)), _resident(win.shape), _resident(wa.shape),
                  _resident(wb.shape), _resident(wout.shape), smem, smem,
                  pos_spec, pos_spec, pos_spec, pos_spec,
                  _resident(dmask.shape), _resident(zeta.shape), _resident(xi.shape)],
        out_specs=x_spec,
        out_shape=jax.ShapeDtypeStruct(x.shape, F32),
        scratch_shapes=[
            pltpu.VMEM((tm, d), BF16),
            pltpu.VMEM((tm, ATT_Q), BF16),
            pltpu.VMEM((tm, ATT_Q), BF16),
            pltpu.VMEM((tm + CHUNK, LANES), BF16),
            pltpu.VMEM((tm + CHUNK, LANES), BF16),
            pltpu.VMEM((tm + CHUNK, LANES), BF16),
            pltpu.VMEM((tm + CHUNK, LANES), BF16),
            pltpu.VMEM((tm, ATT_Q), BF16),
            pltpu.VMEM((tm, RET_QK), BF16),
            pltpu.VMEM((tm, RET_QK), BF16),
            pltpu.VMEM((tm, RET_V), F32),
            pltpu.VMEM((tm, RET_V), F32),
            pltpu.VMEM((RET_HEADS, RET_QK_DIM, RET_V_DIM), F32),
            pltpu.VMEM((tm, d), F32),
        ],
        compiler_params=pltpu.CompilerParams(
            dimension_semantics=("arbitrary", "arbitrary"),
            vmem_limit_bytes=VMEM_LIMIT_BYTES),
        name="mixer",
    )(x, g, win, wa, wb, wout, sinks, gch, cosa, sina, cosr, sinr, dmask, zeta, xi)


def _mlp_call(x2d, g, wup, wdown, gfin, final_norm):
    n, d = x2d.shape
    tm = min(MLP_TILE, n)
    x_spec = pl.BlockSpec((tm, d), lambda i: (i, 0))
    return pl.pallas_call(
        functools.partial(_mlp_kernel, final_norm=final_norm),
        grid=(n // tm,),
        in_specs=[x_spec, _resident((1, d)), _resident(wup.shape), _resident(wdown.shape),
                  _resident((1, d))],
        out_specs=x_spec,
        out_shape=jax.ShapeDtypeStruct(x2d.shape, F32),
        compiler_params=pltpu.CompilerParams(
            dimension_semantics=("arbitrary",),
            vmem_limit_bytes=VMEM_LIMIT_BYTES),
        name="mlp_final" if final_norm else "mlp",
    )(x2d, g, wup, wdown, gfin)


def _position_tables(seq):
    pos = jnp.arange(seq, dtype=F32)
    half = HEAD_DIM // 2
    inv = ROPE_THETA ** (-jnp.arange(half, dtype=F32) / half)
    ang = pos[:, None] * inv[None, :]
    cos, sin = jnp.cos(ang), jnp.sin(ang)
    cosa = jnp.concatenate([cos, cos, cos, cos], axis=1)
    sina = jnp.concatenate([-sin, sin, -sin, sin], axis=1)

    theta = 1.0 / (RET_THETA ** jnp.linspace(0.0, 1.0, RET_QK_DIM // 2, dtype=F32))
    theta = jnp.repeat(theta, 2)
    angr = pos[:, None] * theta[None, :]
    sign = jnp.where(jnp.arange(RET_QK_DIM) % 2 == 0, -1.0, 1.0).astype(F32)
    cosr = jnp.cos(angr)
    sinr = jnp.sin(angr) * sign[None, :]

    log_g = jnp.log(1.0 - 2.0 ** (-5.0 - jnp.arange(RET_HEADS, dtype=F32)))
    idx = jnp.arange(RET_CHUNK, dtype=F32)
    rel = idx[:, None] - idx[None, :]
    dmask = jnp.where(rel[None] >= 0,
                      jnp.exp(log_g[:, None, None] * jnp.maximum(rel, 0.0)[None]), 0.0)
    zeta = jnp.exp(log_g[:, None] * (RET_CHUNK - 1.0 - idx)[None, :])
    xi = jnp.exp(log_g[:, None] * (idx + 1.0)[None, :])
    zeta = jnp.broadcast_to(zeta[:, :, None], (RET_HEADS, RET_CHUNK, LANES))
    xi = jnp.broadcast_to(xi[:, :, None], (RET_HEADS, RET_CHUNK, LANES))
    gch = jnp.exp(log_g * RET_CHUNK)
    return (cosa, sina, cosr, sinr, dmask, zeta, xi), gch


def kernel(x, g_mix, w_in, sinks, w_a, w_b, w_out, g_mlp, w_up, w_down, g_final):
    bsz, seq, d = x.shape
    depth = w_in.shape[0]
    tables, gch = _position_tables(seq)
    for l in range(depth):
        x = _mixer_call(x, g_mix[l][None, :], w_in[l].astype(BF16), w_a[l].astype(BF16),
                        w_b[l].astype(BF16), w_out[l].astype(BF16), sinks[l], gch, tables)
        x = _mlp_call(x.reshape(bsz * seq, d), g_mlp[l][None, :], w_up[l].astype(BF16),
                      w_down[l].astype(BF16), g_final[None, :],
                      final_norm=(l == depth - 1)).reshape(bsz, seq, d)
    return x
```

```python
import functools

import numpy as np
import jax
import jax.numpy as jnp
from jax import lax
from jax.experimental import pallas as pl
from jax.experimental.pallas import tpu as pltpu

D_MODEL = 1024
HEAD_DIM = 64
N_Q_HEADS = 8
N_KV_HEADS = 2
GQA_GROUP = N_Q_HEADS // N_KV_HEADS
WINDOW = 128
ROPE_THETA = 10000.0
RET_HEADS = 4
RET_QK_DIM = 128
RET_V_DIM = 2 * RET_QK_DIM
RET_THETA = 10000.0
D_FF = 4 * D_MODEL
EPS = 1e-6

ATT_Q = N_Q_HEADS * HEAD_DIM
ATT_KV = N_KV_HEADS * HEAD_DIM
RET_QK = RET_HEADS * RET_QK_DIM
RET_V = RET_HEADS * RET_V_DIM

OFF_AQ = 0
OFF_AK = OFF_AQ + ATT_Q
OFF_AV = OFF_AK + ATT_KV
OFF_RQ = OFF_AV + ATT_KV
OFF_RK = OFF_RQ + RET_QK
OFF_RV = OFF_RK + RET_QK
OFF_RG = OFF_RV + RET_V
OFF_GA = OFF_RG + RET_V
OFF_GB = OFF_GA + D_MODEL
W_IN = OFF_GB + D_MODEL

T_AQ = 0
T_AV = T_AQ + ATT_Q
T_RQ = T_AV + ATT_KV
T_RV = T_RQ + RET_QK
T_RG = T_RV + RET_V
T_GA = T_RG + RET_V
T_GB = T_GA + D_MODEL
T_END = T_GB + D_MODEL

LANES = 128
HALF = HEAD_DIM // 2
ATT_BLOCK = WINDOW
RET_BLOCK = 256
SEQ_TILE = 512
MLP_TILE = 512
FF_CHUNK = 1024
VMEM_LIMIT_BYTES = 56 * 1024 * 1024

BF16 = jnp.bfloat16
F32 = jnp.float32


def _dot(a, b):
    return jnp.dot(a, b, preferred_element_type=F32)


def _dot_nt(a, b):
    return lax.dot_general(a, b, (((1,), (1,)), ((), ())), preferred_element_type=F32)


def _dot_tn(a, b):
    return lax.dot_general(a, b, (((0,), (0,)), ((), ())), preferred_element_type=F32)


def _sigmoid(x):
    return 1.0 / (1.0 + jnp.exp(-x))


def _rmsnorm(x, g):
    return x * lax.rsqrt(jnp.mean(x * x, axis=-1, keepdims=True) + EPS) * g


def _mixer_kernel(x_ref, g_ref, wt_ref, wk_ref, wat_ref, wbt_ref, wout_ref, sinks_ref, gch_ref,
                  cqa_ref, sqa_ref, cka_ref, ska_ref, cqr_ref, sqr_ref, ckr_ref, skr_ref,
                  dmask_ref, zeta_ref, xi_ref,
                  o_ref,
                  h_scr, qpad_scr, kband_scr, vt_scr, oatt_scr,
                  rq_scr, rk_scr, rv_scr, r_scr, state_scr, mixed_scr):
    tm = x_ref.shape[1]
    t = pl.program_id(1)

    @pl.when(t == 0)
    def _():
        kband_scr[0:ATT_BLOCK, :] = jnp.zeros((ATT_BLOCK, LANES), BF16)
        vt_scr[:, 0:ATT_BLOCK] = jnp.zeros((ATT_KV, ATT_BLOCK), BF16)
        qpad_scr[...] = jnp.zeros(qpad_scr.shape, BF16)
        state_scr[...] = jnp.zeros(state_scr.shape, F32)

    x = x_ref[0]
    h_scr[...] = _rmsnorm(x, g_ref[...]).astype(BF16)
    h = h_scr[...]

    zk = _dot(h, wk_ref[...])

    qt = _dot_nt(wt_ref[T_AQ:T_AV, :], h)
    vt_scr[:, ATT_BLOCK:ATT_BLOCK + tm] = _dot_nt(wt_ref[T_AV:T_RQ, :], h).astype(BF16)
    ka = zk[:, 0:ATT_KV]
    kband_scr[ATT_BLOCK:ATT_BLOCK + tm, :] = (
        ka * cka_ref[...] + pltpu.roll(ka, LANES // 2, 1) * ska_ref[...]).astype(BF16)

    cq = cqa_ref[...]
    sq = sqa_ref[...]
    scale = HEAD_DIM ** -0.5
    for hh in range(N_Q_HEADS):
        j, g = divmod(hh, GQA_GROUP)
        x1 = qt[HEAD_DIM * hh:HEAD_DIM * hh + HALF]
        x2 = qt[HEAD_DIM * hh + HALF:HEAD_DIM * (hh + 1)]
        qpad_scr[j, HALF * j:HALF * (j + 1), g * tm:(g + 1) * tm] = (
            (x1 * cq - x2 * sq) * scale).astype(BF16)
        qpad_scr[j, 2 * HALF + HALF * j:2 * HALF + HALF * (j + 1), g * tm:(g + 1) * tm] = (
            (x2 * cq + x1 * sq) * scale).astype(BF16)

    ncol = GQA_GROUP * ATT_BLOCK
    kj = lax.broadcasted_iota(jnp.int32, (2 * ATT_BLOCK, ncol), 0)
    qi = lax.broadcasted_iota(jnp.int32, (2 * ATT_BLOCK, ncol), 1) % ATT_BLOCK
    band = (kj > qi) & (kj <= qi + WINDOW)
    band_first = band & ((kj >= ATT_BLOCK) | (t > 0))
    sink_rows = [
        jnp.concatenate([jnp.full((1, ATT_BLOCK), sinks_ref[GQA_GROUP * j + g], F32)
                         for g in range(GQA_GROUP)], axis=1) for j in range(N_KV_HEADS)]

    for c in range(tm // ATT_BLOCK):
        r0 = c * ATT_BLOCK
        mask = band_first if c == 0 else band
        kb = kband_scr[r0:r0 + 2 * ATT_BLOCK, :]
        for j in range(N_KV_HEADS):
            qs = jnp.concatenate(
                [qpad_scr[j, :, g * tm + r0:g * tm + r0 + ATT_BLOCK] for g in range(GQA_GROUP)],
                axis=1)
            st = jnp.where(mask, _dot(kb, qs), -jnp.inf)
            m = jnp.maximum(jnp.max(st, axis=0, keepdims=True), sink_rows[j])
            p = jnp.exp(st - m)
            denom = jnp.sum(p, axis=0, keepdims=True) + jnp.exp(sink_rows[j] - m)
            vb = vt_scr[HEAD_DIM * j:HEAD_DIM * (j + 1), r0:r0 + 2 * ATT_BLOCK]
            ot = _dot(vb, p.astype(BF16)) / denom
            for g in range(GQA_GROUP):
                hh = GQA_GROUP * j + g
                oatt_scr[HEAD_DIM * hh:HEAD_DIM * (hh + 1), r0:r0 + ATT_BLOCK] = (
                    ot[:, g * ATT_BLOCK:(g + 1) * ATT_BLOCK].astype(BF16))

    kband_scr[0:ATT_BLOCK, :] = kband_scr[tm:tm + ATT_BLOCK, :]
    vt_scr[:, 0:ATT_BLOCK] = vt_scr[:, tm:tm + ATT_BLOCK]

    yat = _dot(wat_ref[...], oatt_scr[...])
    gat = _dot_nt(wt_ref[T_GA:T_GB, :], h)
    mixed_scr[...] = _sigmoid(gat) * yat

    rqt = _dot_nt(wt_ref[T_RQ:T_RV, :], h)
    rv_scr[...] = _dot_nt(wt_ref[T_RV:T_RG, :], h)
    cqr = cqr_ref[...]
    sqr = sqr_ref[...]
    ckr = ckr_ref[...]
    skr = skr_ref[...]
    kscale = RET_QK_DIM ** -0.5
    hq = RET_QK_DIM // 2
    for hd in range(RET_HEADS):
        ev = rqt[RET_QK_DIM * hd:RET_QK_DIM * hd + hq]
        od = rqt[RET_QK_DIM * hd + hq:RET_QK_DIM * (hd + 1)]
        rq_scr[RET_QK_DIM * hd:RET_QK_DIM * hd + hq, :] = (ev * cqr - od * sqr).astype(BF16)
        rq_scr[RET_QK_DIM * hd + hq:RET_QK_DIM * (hd + 1), :] = (od * cqr + ev * sqr).astype(BF16)
        kk = zk[:, ATT_KV + RET_QK_DIM * hd:ATT_KV + RET_QK_DIM * (hd + 1)]
        rk_scr[:, RET_QK_DIM * hd:RET_QK_DIM * (hd + 1)] = (
            (kk * ckr + pltpu.roll(kk, LANES // 2, 1) * skr) * kscale).astype(BF16)

    for c in range(tm // RET_BLOCK):
        r0 = c * RET_BLOCK
        for hd in range(RET_HEADS):
            q = rq_scr[RET_QK_DIM * hd:RET_QK_DIM * (hd + 1), r0:r0 + RET_BLOCK]
            kk = rk_scr[r0:r0 + RET_BLOCK, RET_QK_DIM * hd:RET_QK_DIM * (hd + 1)]
            vv = rv_scr[RET_V_DIM * hd:RET_V_DIM * (hd + 1), r0:r0 + RET_BLOCK]
            state = state_scr[hd]
            s = _dot(kk, q) * dmask_ref[hd]
            inner = _dot(vv.astype(BF16), s.astype(BF16))
            cross = _dot(state.astype(BF16), q) * xi_ref[hd][0:1, :]
            out = inner + cross
            kv = _dot((vv * zeta_ref[hd][0:1, :]).astype(BF16), kk)
            state_scr[hd] = state * gch_ref[hd] + kv
            r_scr[RET_V_DIM * hd:RET_V_DIM * (hd + 1), r0:r0 + RET_BLOCK] = (
                out * lax.rsqrt(jnp.mean(out * out, axis=0, keepdims=True) + EPS))

    rgt = _dot_nt(wt_ref[T_RG:T_GA, :], h)
    rbt = (r_scr[...] * (rgt * _sigmoid(rgt))).astype(BF16)
    ybt = _dot(wbt_ref[...], rbt)
    gbt = _dot_nt(wt_ref[T_GB:T_END, :], h)
    mixed = mixed_scr[...] + _sigmoid(gbt) * ybt
    o_ref[0] = x_ref[0] + _dot_tn(mixed.astype(BF16), wout_ref[...])


def _mlp_kernel(x_ref, g_ref, wup_ref, wdown_ref, gfin_ref, o_ref, *, final_norm):
    x = x_ref[...]
    h = _rmsnorm(x, g_ref[...]).astype(BF16)
    acc = jnp.zeros(x.shape, F32)
    for j in range(D_FF // FF_CHUNK):
        u = _dot(h, wup_ref[:, j * FF_CHUNK:(j + 1) * FF_CHUNK])
        a = jnp.square(jnp.maximum(u, 0.0)).astype(BF16)
        acc = acc + _dot(a, wdown_ref[j * FF_CHUNK:(j + 1) * FF_CHUNK, :])
    y = x + acc
    if final_norm:
        y = _rmsnorm(y, gfin_ref[...])
    o_ref[...] = y


def _resident(shape):
    return pl.BlockSpec(shape, lambda *_: (0,) * len(shape), pipeline_mode=pl.Buffered(1))


def _mixer_call(x, g, wt, wk, wat, wbt, wout, sinks, gch, tables):
    bsz, seq, d = x.shape
    tm = min(SEQ_TILE, seq)
    cqa, sqa, cka, ska, cqr, sqr, ckr, skr, dmask, zeta, xi = tables
    tok_spec = pl.BlockSpec((tm, LANES), lambda b, t: (t, 0))

    def feat_spec(rows):
        return pl.BlockSpec((rows, tm), lambda b, t: (0, t))

    smem = pl.BlockSpec(memory_space=pltpu.SMEM)
    x_spec = pl.BlockSpec((1, tm, d), lambda b, t: (b, t, 0))
    return pl.pallas_call(
        _mixer_kernel,
        grid=(bsz, seq // tm),
        in_specs=[x_spec, _resident((1, d)), _resident(wt.shape), _resident(wk.shape),
                  _resident(wat.shape), _resident(wbt.shape), _resident(wout.shape), smem, smem,
                  feat_spec(HALF), feat_spec(HALF), tok_spec, tok_spec,
                  feat_spec(RET_QK_DIM // 2), feat_spec(RET_QK_DIM // 2), tok_spec, tok_spec,
                  _resident(dmask.shape), _resident(zeta.shape), _resident(xi.shape)],
        out_specs=x_spec,
        out_shape=jax.ShapeDtypeStruct(x.shape, F32),
        scratch_shapes=[
            pltpu.VMEM((tm, d), BF16),
            pltpu.VMEM((N_KV_HEADS, LANES, GQA_GROUP * tm), BF16),
            pltpu.VMEM((tm + ATT_BLOCK, LANES), BF16),
            pltpu.VMEM((ATT_KV, tm + ATT_BLOCK), BF16),
            pltpu.VMEM((ATT_Q, tm), BF16),
            pltpu.VMEM((RET_QK, tm), BF16),
            pltpu.VMEM((tm, RET_QK), BF16),
            pltpu.VMEM((RET_V, tm), F32),
            pltpu.VMEM((RET_V, tm), F32),
            pltpu.VMEM((RET_HEADS, RET_V_DIM, RET_QK_DIM), F32),
            pltpu.VMEM((d, tm), F32),
        ],
        compiler_params=pltpu.CompilerParams(
            dimension_semantics=("arbitrary", "arbitrary"),
            vmem_limit_bytes=VMEM_LIMIT_BYTES),
        name="mixer",
    )(x, g, wt, wk, wat, wbt, wout, sinks, gch, cqa, sqa, cka, ska, cqr, sqr, ckr, skr,
      dmask, zeta, xi)


def _mlp_call(x2d, g, wup, wdown, gfin, final_norm):
    n, d = x2d.shape
    tm = min(MLP_TILE, n)
    x_spec = pl.BlockSpec((tm, d), lambda i: (i, 0))
    return pl.pallas_call(
        functools.partial(_mlp_kernel, final_norm=final_norm),
        grid=(n // tm,),
        in_specs=[x_spec, _resident((1, d)), _resident(wup.shape), _resident(wdown.shape),
                  _resident((1, d))],
        out_specs=x_spec,
        out_shape=jax.ShapeDtypeStruct(x2d.shape, F32),
        compiler_params=pltpu.CompilerParams(
            dimension_semantics=("arbitrary",),
            vmem_limit_bytes=VMEM_LIMIT_BYTES),
        name="mlp_final" if final_norm else "mlp",
    )(x2d, g, wup, wdown, gfin)


_AK_PERM = np.concatenate([np.arange(0, HALF), np.arange(2 * HALF, 3 * HALF),
                           np.arange(HALF, 2 * HALF), np.arange(3 * HALF, 4 * HALF)])
_RET_PERM = np.concatenate([
    np.concatenate([h * RET_QK_DIM + np.arange(0, RET_QK_DIM, 2),
                    h * RET_QK_DIM + np.arange(1, RET_QK_DIM, 2)]) for h in range(RET_HEADS)])


def _split_in_projection(w_in):
    wt = jnp.concatenate([
        w_in[:, OFF_AQ:OFF_AK], w_in[:, OFF_AV:OFF_RQ],
        w_in[:, OFF_RQ:OFF_RK][:, _RET_PERM], w_in[:, OFF_RV:W_IN]], axis=1)
    wk = jnp.concatenate([w_in[:, OFF_AK:OFF_AV][:, _AK_PERM],
                          w_in[:, OFF_RK:OFF_RV][:, _RET_PERM]], axis=1)
    return wt.T.astype(BF16), wk.astype(BF16)


def _position_tables(seq):
    pos = jnp.arange(seq, dtype=F32)
    inv = ROPE_THETA ** (-jnp.arange(HALF, dtype=F32) / HALF)
    ang = pos[:, None] * inv[None, :]
    cos, sin = jnp.cos(ang), jnp.sin(ang)
    cka = jnp.concatenate([cos, cos, cos, cos], axis=1)
    ska = jnp.concatenate([-sin, -sin, sin, sin], axis=1)

    theta = 1.0 / (RET_THETA ** jnp.linspace(0.0, 1.0, RET_QK_DIM // 2, dtype=F32))
    angr = pos[:, None] * theta[None, :]
    cosr, sinr = jnp.cos(angr), jnp.sin(angr)
    ckr = jnp.concatenate([cosr, cosr], axis=1)
    skr = jnp.concatenate([-sinr, sinr], axis=1)

    log_g = jnp.log(1.0 - 2.0 ** (-5.0 - jnp.arange(RET_HEADS, dtype=F32)))
    idx = jnp.arange(RET_BLOCK, dtype=F32)
    rel = idx[None, :] - idx[:, None]
    dmask = jnp.where(rel[None] >= 0,
                      jnp.exp(log_g[:, None, None] * jnp.maximum(rel, 0.0)[None]), 0.0)
    zeta = jnp.exp(log_g[:, None] * (RET_BLOCK - 1.0 - idx)[None, :])
    xi = jnp.exp(log_g[:, None] * (idx + 1.0)[None, :])
    zeta = jnp.broadcast_to(zeta[:, None, :], (RET_HEADS, 8, RET_BLOCK))
    xi = jnp.broadcast_to(xi[:, None, :], (RET_HEADS, 8, RET_BLOCK))
    gch = jnp.exp(log_g * RET_BLOCK)
    return (cos.T, sin.T, cka, ska, cosr.T, sinr.T, ckr, skr, dmask, zeta, xi), gch


def kernel(x, g_mix, w_in, sinks, w_a, w_b, w_out, g_mlp, w_up, w_down, g_final):
    bsz, seq, d = x.shape
    depth = w_in.shape[0]
    tables, gch = _position_tables(seq)
    for l in range(depth):
        wt, wk = _split_in_projection(w_in[l])
        x = _mixer_call(x, g_mix[l][None, :], wt, wk, w_a[l].T.astype(BF16), w_b[l].T.astype(BF16),
                        w_out[l].astype(BF16), sinks[l], gch, tables)
        x = _mlp_call(x.reshape(bsz * seq, d), g_mlp[l][None, :], w_up[l].astype(BF16),
                      w_down[l].astype(BF16), g_final[None, :],
                      final_norm=(l == depth - 1)).reshape(bsz, seq, d)
    return x
```

```python
import functools

import numpy as np
import jax
import jax.numpy as jnp
from jax import lax
from jax.experimental import pallas as pl
from jax.experimental.pallas import tpu as pltpu

D_MODEL = 1024
HEAD_DIM = 64
N_Q_HEADS = 8
N_KV_HEADS = 2
GQA_GROUP = N_Q_HEADS // N_KV_HEADS
WINDOW = 128
ROPE_THETA = 10000.0
RET_HEADS = 4
RET_QK_DIM = 128
RET_V_DIM = 2 * RET_QK_DIM
RET_THETA = 10000.0
D_FF = 4 * D_MODEL
EPS = 1e-6

ATT_Q = N_Q_HEADS * HEAD_DIM
ATT_KV = N_KV_HEADS * HEAD_DIM
RET_QK = RET_HEADS * RET_QK_DIM
RET_V = RET_HEADS * RET_V_DIM

OFF_AQ = 0
OFF_AK = OFF_AQ + ATT_Q
OFF_AV = OFF_AK + ATT_KV
OFF_RQ = OFF_AV + ATT_KV
OFF_RK = OFF_RQ + RET_QK
OFF_RV = OFF_RK + RET_QK
OFF_RG = OFF_RV + RET_V
OFF_GA = OFF_RG + RET_V
OFF_GB = OFF_GA + D_MODEL
W_IN = OFF_GB + D_MODEL

T_AQ = 0
T_AV = T_AQ + ATT_Q
T_RQ = T_AV + ATT_KV
T_RV = T_RQ + RET_QK
T_RG = T_RV + RET_V
T_GA = T_RG + RET_V
T_GB = T_GA + D_MODEL
T_END = T_GB + D_MODEL

LANES = 128
HALF = HEAD_DIM // 2
ATT_BLOCK = WINDOW
RET_BLOCK = 256
SEQ_TILE = 512
ROW_BLOCK = 256
MLP_TILE = 512
FF_CHUNK = 1024
VMEM_LIMIT_BYTES = 56 * 1024 * 1024

BF16 = jnp.bfloat16
F32 = jnp.float32


def _dot(a, b):
    return jnp.dot(a, b, preferred_element_type=F32)


def _dot_nt(a, b):
    return lax.dot_general(a, b, (((1,), (1,)), ((), ())), preferred_element_type=F32)


def _dot_tn(a, b):
    return lax.dot_general(a, b, (((0,), (0,)), ((), ())), preferred_element_type=F32)


def _sigmoid(x):
    return 1.0 / (1.0 + jnp.exp(-x))


def _rmsnorm(x, g):
    return x * lax.rsqrt(jnp.mean(x * x, axis=-1, keepdims=True) + EPS) * g


def _mixer_kernel(x_ref, g_ref, wt_ref, wk_ref, wat_ref, wbt_ref, wout_ref, sinks_ref, gch_ref,
                  cqa_ref, sqa_ref, cka_ref, ska_ref, cqr_ref, sqr_ref, ckr_ref, skr_ref,
                  dmask_ref, zeta_ref, xi_ref,
                  o_ref,
                  h_scr, ht_scr, qpad_scr, kband_scr, vt_scr, oatt_scr,
                  rq_scr, rk_scr, rv_scr, silu_scr, rb_scr, state_scr, mixed_scr, sgb_scr, mixedb_scr):
    tm = x_ref.shape[1]
    t = pl.program_id(1)

    @pl.when(t == 0)
    def _():
        kband_scr[0:ATT_BLOCK, :] = jnp.zeros((ATT_BLOCK, LANES), BF16)
        vt_scr[:, 0:ATT_BLOCK] = jnp.zeros((ATT_KV, ATT_BLOCK), BF16)
        qpad_scr[...] = jnp.zeros(qpad_scr.shape, BF16)
        state_scr[...] = jnp.zeros(state_scr.shape, F32)

    x = x_ref[0]
    hf = _rmsnorm(x, g_ref[...])
    h_scr[...] = hf.astype(BF16)
    ht_scr[...] = hf.T.astype(BF16)
    h = h_scr[...]
    ht = ht_scr[...]

    zk = _dot(h, wk_ref[...])

    def proj(r0, nrows):
        return _dot(wt_ref[r0:r0 + nrows, :], ht)

    kscale = RET_QK_DIM ** -0.5
    hq = RET_QK_DIM // 2

    def item_rk(hd):
        kk = zk[:, ATT_KV + RET_QK_DIM * hd:ATT_KV + RET_QK_DIM * (hd + 1)]
        rk_scr[:, RET_QK_DIM * hd:RET_QK_DIM * (hd + 1)] = (
            (kk * ckr_ref[...] + pltpu.roll(kk, LANES // 2, 1) * skr_ref[...]) * kscale).astype(BF16)

    def item_rq(i):
        blk = proj(T_RQ + i * ROW_BLOCK, ROW_BLOCK)
        cqr = cqr_ref[...]
        sqr = sqr_ref[...]
        for hl in range(ROW_BLOCK // RET_QK_DIM):
            base = i * ROW_BLOCK + hl * RET_QK_DIM
            ev = blk[hl * RET_QK_DIM:hl * RET_QK_DIM + hq]
            od = blk[hl * RET_QK_DIM + hq:(hl + 1) * RET_QK_DIM]
            rq_scr[base:base + hq, :] = (ev * cqr - od * sqr).astype(BF16)
            rq_scr[base + hq:base + RET_QK_DIM, :] = (od * cqr + ev * sqr).astype(BF16)

    def item_rv(i):
        rv_scr[i * ROW_BLOCK:(i + 1) * ROW_BLOCK, :] = proj(T_RV + i * ROW_BLOCK, ROW_BLOCK)

    def item_rg(i):
        rg = proj(T_RG + i * ROW_BLOCK, ROW_BLOCK)
        silu_scr[i * ROW_BLOCK:(i + 1) * ROW_BLOCK, :] = rg * _sigmoid(rg)

    def item_ga(i):
        ga = proj(T_GA + i * ROW_BLOCK, ROW_BLOCK)
        ya = _dot(wat_ref[i * ROW_BLOCK:(i + 1) * ROW_BLOCK, :], oatt_scr[...])
        mixed_scr[i * ROW_BLOCK:(i + 1) * ROW_BLOCK, :] = _sigmoid(ga) * ya

    def item_gb(i):
        sgb_scr[i * ROW_BLOCK:(i + 1) * ROW_BLOCK, :] = _sigmoid(proj(T_GB + i * ROW_BLOCK, ROW_BLOCK))

    def spread(items, n_units):
        cuts = [round(u * len(items) / n_units) for u in range(n_units + 1)]
        return [items[cuts[u]:cuts[u + 1]] for u in range(n_units)]

    nblk = D_MODEL // ROW_BLOCK
    att_items = ([functools.partial(item_rk, hd) for hd in range(RET_HEADS)]
                 + [functools.partial(item_rq, i) for i in range(RET_QK // ROW_BLOCK)]
                 + [functools.partial(item_rv, i) for i in range(RET_V // ROW_BLOCK)]
                 + [functools.partial(item_rg, i) for i in range(RET_V // ROW_BLOCK)])
    ret_items = ([functools.partial(item_ga, i) for i in range(nblk)]
                 + [functools.partial(item_gb, i) for i in range(nblk)])

    qt = proj(T_AQ, ATT_Q)
    vt_scr[:, ATT_BLOCK:ATT_BLOCK + tm] = proj(T_AV, ATT_KV).astype(BF16)
    ka = zk[:, 0:ATT_KV]
    kband_scr[ATT_BLOCK:ATT_BLOCK + tm, :] = (
        ka * cka_ref[...] + pltpu.roll(ka, LANES // 2, 1) * ska_ref[...]).astype(BF16)

    cq = cqa_ref[...]
    sq = sqa_ref[...]
    scale = HEAD_DIM ** -0.5
    for hh in range(N_Q_HEADS):
        j, g = divmod(hh, GQA_GROUP)
        x1 = qt[HEAD_DIM * hh:HEAD_DIM * hh + HALF]
        x2 = qt[HEAD_DIM * hh + HALF:HEAD_DIM * (hh + 1)]
        qpad_scr[j, HALF * j:HALF * (j + 1), g * tm:(g + 1) * tm] = (
            (x1 * cq - x2 * sq) * scale).astype(BF16)
        qpad_scr[j, 2 * HALF + HALF * j:2 * HALF + HALF * (j + 1), g * tm:(g + 1) * tm] = (
            (x2 * cq + x1 * sq) * scale).astype(BF16)

    ncol = GQA_GROUP * ATT_BLOCK
    kj = lax.broadcasted_iota(jnp.int32, (2 * ATT_BLOCK, ncol), 0)
    qi = lax.broadcasted_iota(jnp.int32, (2 * ATT_BLOCK, ncol), 1) % ATT_BLOCK
    band = (kj > qi) & (kj <= qi + WINDOW)
    band_first = band & ((kj >= ATT_BLOCK) | (t > 0))
    sink_rows = [
        jnp.concatenate([jnp.full((1, ATT_BLOCK), sinks_ref[GQA_GROUP * j + g], F32)
                         for g in range(GQA_GROUP)], axis=1) for j in range(N_KV_HEADS)]

    def att_scores(c, j):
        r0 = c * ATT_BLOCK
        qs = jnp.concatenate(
            [qpad_scr[j, :, g * tm + r0:g * tm + r0 + ATT_BLOCK] for g in range(GQA_GROUP)],
            axis=1)
        kb = kband_scr[r0:r0 + 2 * ATT_BLOCK, :]
        return jnp.where(band_first if c == 0 else band, _dot(kb, qs), -jnp.inf)

    def att_finish(c, j, st):
        r0 = c * ATT_BLOCK
        m = jnp.maximum(jnp.max(st, axis=0, keepdims=True), sink_rows[j])
        p = jnp.exp(st - m)
        denom = jnp.sum(p, axis=0, keepdims=True) + jnp.exp(sink_rows[j] - m)
        vb = vt_scr[HEAD_DIM * j:HEAD_DIM * (j + 1), r0:r0 + 2 * ATT_BLOCK]
        ot = _dot(vb, p.astype(BF16)) / denom
        for g in range(GQA_GROUP):
            hh = GQA_GROUP * j + g
            oatt_scr[HEAD_DIM * hh:HEAD_DIM * (hh + 1), r0:r0 + ATT_BLOCK] = (
                ot[:, g * ATT_BLOCK:(g + 1) * ATT_BLOCK].astype(BF16))

    att_units = [(c, j) for c in range(tm // ATT_BLOCK) for j in range(N_KV_HEADS)]
    att_sched = spread(att_items, len(att_units))
    st_next = att_scores(*att_units[0])
    for u, unit in enumerate(att_units):
        st_cur = st_next
        if u + 1 < len(att_units):
            st_next = att_scores(*att_units[u + 1])
        for item in att_sched[u]:
            item()
        att_finish(*unit, st_cur)

    kband_scr[0:ATT_BLOCK, :] = kband_scr[tm:tm + ATT_BLOCK, :]
    vt_scr[:, 0:ATT_BLOCK] = vt_scr[:, tm:tm + ATT_BLOCK]

    def ret_scores(c, hd):
        r0 = c * RET_BLOCK
        q = rq_scr[RET_QK_DIM * hd:RET_QK_DIM * (hd + 1), r0:r0 + RET_BLOCK]
        kk = rk_scr[r0:r0 + RET_BLOCK, RET_QK_DIM * hd:RET_QK_DIM * (hd + 1)]
        return (_dot(kk, q) * dmask_ref[hd]).astype(BF16)

    def ret_finish(c, hd, s):
        r0 = c * RET_BLOCK
        q = rq_scr[RET_QK_DIM * hd:RET_QK_DIM * (hd + 1), r0:r0 + RET_BLOCK]
        kk = rk_scr[r0:r0 + RET_BLOCK, RET_QK_DIM * hd:RET_QK_DIM * (hd + 1)]
        vv = rv_scr[RET_V_DIM * hd:RET_V_DIM * (hd + 1), r0:r0 + RET_BLOCK]
        state = state_scr[hd]
        inner = _dot(vv.astype(BF16), s)
        cross = _dot(state.astype(BF16), q) * xi_ref[hd][0:1, :]
        out = inner + cross
        kv = _dot((vv * zeta_ref[hd][0:1, :]).astype(BF16), kk)
        state_scr[hd] = state * gch_ref[hd] + kv
        rn = out * lax.rsqrt(jnp.mean(out * out, axis=0, keepdims=True) + EPS)
        rb_scr[RET_V_DIM * hd:RET_V_DIM * (hd + 1), r0:r0 + RET_BLOCK] = (
            rn * silu_scr[RET_V_DIM * hd:RET_V_DIM * (hd + 1), r0:r0 + RET_BLOCK]).astype(BF16)

    ret_units = [(c, hd) for c in range(tm // RET_BLOCK) for hd in range(RET_HEADS)]
    ret_sched = spread(ret_items, len(ret_units))
    s_next = ret_scores(*ret_units[0])
    for u, unit in enumerate(ret_units):
        s_cur = s_next
        if u + 1 < len(ret_units):
            s_next = ret_scores(*ret_units[u + 1])
        for item in ret_sched[u]:
            item()
        ret_finish(*unit, s_cur)

    rb = rb_scr[...]
    for i in range(nblk):
        rows = slice(i * ROW_BLOCK, (i + 1) * ROW_BLOCK)
        yb = _dot(wbt_ref[rows, :], rb)
        mixedb_scr[rows, :] = (mixed_scr[rows, :] + sgb_scr[rows, :] * yb).astype(BF16)
    o_ref[0] = x_ref[0] + _dot_tn(mixedb_scr[...], wout_ref[...])


def _mlp_kernel(x_ref, g_ref, wup_ref, wdown_ref, gfin_ref, o_ref, *, final_norm):
    x = x_ref[...]
    h = _rmsnorm(x, g_ref[...]).astype(BF16)
    acc = jnp.zeros(x.shape, F32)
    for j in range(D_FF // FF_CHUNK):
        u = _dot(h, wup_ref[:, j * FF_CHUNK:(j + 1) * FF_CHUNK])
        a = jnp.square(jnp.maximum(u, 0.0)).astype(BF16)
        acc = acc + _dot(a, wdown_ref[j * FF_CHUNK:(j + 1) * FF_CHUNK, :])
    y = x + acc
    if final_norm:
        y = _rmsnorm(y, gfin_ref[...])
    o_ref[...] = y


def _resident(shape):
    return pl.BlockSpec(shape, lambda *_: (0,) * len(shape), pipeline_mode=pl.Buffered(1))


def _mixer_call(x, g, wt, wk, wat, wbt, wout, sinks, gch, tables):
    bsz, seq, d = x.shape
    tm = min(SEQ_TILE, seq)
    cqa, sqa, cka, ska, cqr, sqr, ckr, skr, dmask, zeta, xi = tables
    tok_spec = pl.BlockSpec((tm, LANES), lambda b, t: (t, 0))

    def feat_spec(rows):
        return pl.BlockSpec((rows, tm), lambda b, t: (0, t))

    smem = pl.BlockSpec(memory_space=pltpu.SMEM)
    x_spec = pl.BlockSpec((1, tm, d), lambda b, t: (b, t, 0))
    return pl.pallas_call(
        _mixer_kernel,
        grid=(bsz, seq // tm),
        in_specs=[x_spec, _resident((1, d)), _resident(wt.shape), _resident(wk.shape),
                  _resident(wat.shape), _resident(wbt.shape), _resident(wout.shape), smem, smem,
                  feat_spec(HALF), feat_spec(HALF), tok_spec, tok_spec,
                  feat_spec(RET_QK_DIM // 2), feat_spec(RET_QK_DIM // 2), tok_spec, tok_spec,
                  _resident(dmask.shape), _resident(zeta.shape), _resident(xi.shape)],
        out_specs=x_spec,
        out_shape=jax.ShapeDtypeStruct(x.shape, F32),
        scratch_shapes=[
            pltpu.VMEM((tm, d), BF16),
            pltpu.VMEM((d, tm), BF16),
            pltpu.VMEM((N_KV_HEADS, LANES, GQA_GROUP * tm), BF16),
            pltpu.VMEM((tm + ATT_BLOCK, LANES), BF16),
            pltpu.VMEM((ATT_KV, tm + ATT_BLOCK), BF16),
            pltpu.VMEM((ATT_Q, tm), BF16),
            pltpu.VMEM((RET_QK, tm), BF16),
            pltpu.VMEM((tm, RET_QK), BF16),
            pltpu.VMEM((RET_V, tm), F32),
            pltpu.VMEM((RET_V, tm), F32),
            pltpu.VMEM((RET_V, tm), BF16),
            pltpu.VMEM((RET_HEADS, RET_V_DIM, RET_QK_DIM), F32),
            pltpu.VMEM((d, tm), F32),
            pltpu.VMEM((d, tm), F32),
            pltpu.VMEM((d, tm), BF16),
        ],
        compiler_params=pltpu.CompilerParams(
            dimension_semantics=("arbitrary", "arbitrary"),
            vmem_limit_bytes=VMEM_LIMIT_BYTES),
        name="mixer",
    )(x, g, wt, wk, wat, wbt, wout, sinks, gch, cqa, sqa, cka, ska, cqr, sqr, ckr, skr,
      dmask, zeta, xi)


def _mlp_call(x2d, g, wup, wdown, gfin, final_norm):
    n, d = x2d.shape
    tm = min(MLP_TILE, n)
    x_spec = pl.BlockSpec((tm, d), lambda i: (i, 0))
    return pl.pallas_call(
        functools.partial(_mlp_kernel, final_norm=final_norm),
        grid=(n // tm,),
        in_specs=[x_spec, _resident((1, d)), _resident(wup.shape), _resident(wdown.shape),
                  _resident((1, d))],
        out_specs=x_spec,
        out_shape=jax.ShapeDtypeStruct(x2d.shape, F32),
        compiler_params=pltpu.CompilerParams(
            dimension_semantics=("arbitrary",),
            vmem_limit_bytes=VMEM_LIMIT_BYTES),
        name="mlp_final" if final_norm else "mlp",
    )(x2d, g, wup, wdown, gfin)


_AK_PERM = np.concatenate([np.arange(0, HALF), np.arange(2 * HALF, 3 * HALF),
                           np.arange(HALF, 2 * HALF), np.arange(3 * HALF, 4 * HALF)])
_RET_PERM = np.concatenate([
    np.concatenate([h * RET_QK_DIM + np.arange(0, RET_QK_DIM, 2),
                    h * RET_QK_DIM + np.arange(1, RET_QK_DIM, 2)]) for h in range(RET_HEADS)])


def _split_in_projection(w_in):
    wt = jnp.concatenate([
        w_in[:, OFF_AQ:OFF_AK], w_in[:, OFF_AV:OFF_RQ],
        w_in[:, OFF_RQ:OFF_RK][:, _RET_PERM], w_in[:, OFF_RV:W_IN]], axis=1)
    wk = jnp.concatenate([w_in[:, OFF_AK:OFF_AV][:, _AK_PERM],
                          w_in[:, OFF_RK:OFF_RV][:, _RET_PERM]], axis=1)
    return wt.T.astype(BF16), wk.astype(BF16)


def _position_tables(seq):
    pos = jnp.arange(seq, dtype=F32)
    inv = ROPE_THETA ** (-jnp.arange(HALF, dtype=F32) / HALF)
    ang = pos[:, None] * inv[None, :]
    cos, sin = jnp.cos(ang), jnp.sin(ang)
    cka = jnp.concatenate([cos, cos, cos, cos], axis=1)
    ska = jnp.concatenate([-sin, -sin, sin, sin], axis=1)

    theta = 1.0 / (RET_THETA ** jnp.linspace(0.0, 1.0, RET_QK_DIM // 2, dtype=F32))
    angr = pos[:, None] * theta[None, :]
    cosr, sinr = jnp.cos(angr), jnp.sin(angr)
    ckr = jnp.concatenate([cosr, cosr], axis=1)
    skr = jnp.concatenate([-sinr, sinr], axis=1)

    log_g = jnp.log(1.0 - 2.0 ** (-5.0 - jnp.arange(RET_HEADS, dtype=F32)))
    idx = jnp.arange(RET_BLOCK, dtype=F32)
    rel = idx[None, :] - idx[:, None]
    dmask = jnp.where(rel[None] >= 0,
                      jnp.exp(log_g[:, None, None] * jnp.maximum(rel, 0.0)[None]), 0.0)
    zeta = jnp.exp(log_g[:, None] * (RET_BLOCK - 1.0 - idx)[None, :])
    xi = jnp.exp(log_g[:, None] * (idx + 1.0)[None, :])
    zeta = jnp.broadcast_to(zeta[:, None, :], (RET_HEADS, 8, RET_BLOCK))
    xi = jnp.broadcast_to(xi[:, None, :], (RET_HEADS, 8, RET_BLOCK))
    gch = jnp.exp(log_g * RET_BLOCK)
    return (cos.T, sin.T, cka, ska, cosr.T, sinr.T, ckr, skr, dmask, zeta, xi), gch


def kernel(x, g_mix, w_in, sinks, w_a, w_b, w_out, g_mlp, w_up, w_down, g_final):
    bsz, seq, d = x.shape
    depth = w_in.shape[0]
    tables, gch = _position_tables(seq)
    for l in range(depth):
        wt, wk = _split_in_projection(w_in[l])
        x = _mixer_call(x, g_mix[l][None, :], wt, wk, w_a[l].T.astype(BF16), w_b[l].T.astype(BF16),
                        w_out[l].astype(BF16), sinks[l], gch, tables)
        x = _mlp_call(x.reshape(bsz * seq, d), g_mlp[l][None, :], w_up[l].astype(BF16),
                      w_down[l].astype(BF16), g_final[None, :],
                      final_norm=(l == depth - 1)).reshape(bsz, seq, d)
    return x
```

```python
import functools

import numpy as np
import jax
import jax.numpy as jnp
from jax import lax
from jax.experimental import pallas as pl
from jax.experimental.pallas import tpu as pltpu

D_MODEL = 1024
HEAD_DIM = 64
N_Q_HEADS = 8
N_KV_HEADS = 2
GQA_GROUP = N_Q_HEADS // N_KV_HEADS
WINDOW = 128
ROPE_THETA = 10000.0
RET_HEADS = 4
RET_QK_DIM = 128
RET_V_DIM = 2 * RET_QK_DIM
RET_THETA = 10000.0
D_FF = 4 * D_MODEL
EPS = 1e-6

ATT_Q = N_Q_HEADS * HEAD_DIM
ATT_KV = N_KV_HEADS * HEAD_DIM
RET_QK = RET_HEADS * RET_QK_DIM
RET_V = RET_HEADS * RET_V_DIM

OFF_AQ = 0
OFF_AK = OFF_AQ + ATT_Q
OFF_AV = OFF_AK + ATT_KV
OFF_RQ = OFF_AV + ATT_KV
OFF_RK = OFF_RQ + RET_QK
OFF_RV = OFF_RK + RET_QK
OFF_RG = OFF_RV + RET_V
OFF_GA = OFF_RG + RET_V
OFF_GB = OFF_GA + D_MODEL
W_IN = OFF_GB + D_MODEL

TOK_CKA, TOK_SKA, TOK_CKR, TOK_SKR = 0, 128, 256, 384
FEAT_CQA = 0
FEAT_SQA = FEAT_CQA + HEAD_DIM // 2
FEAT_CQR = FEAT_SQA + HEAD_DIM // 2
FEAT_SQR = FEAT_CQR + RET_QK_DIM // 2
FEAT_END = FEAT_SQR + RET_QK_DIM // 2

LANES = 128
HALF = HEAD_DIM // 2
ATT_BLOCK = WINDOW
RET_BLOCK = 256
SEQ_TILE = 512
ROW_BLOCK = 256
MLP_TILE = 512
FF_CHUNK = 1024
VMEM_LIMIT_BYTES = 56 * 1024 * 1024

BF16 = jnp.bfloat16
F32 = jnp.float32


def _dot(a, b):
    return jnp.dot(a, b, preferred_element_type=F32)


def _dot_nt(a, b):
    return lax.dot_general(a, b, (((1,), (1,)), ((), ())), preferred_element_type=F32)


def _dot_tn(a, b):
    return lax.dot_general(a, b, (((0,), (0,)), ((), ())), preferred_element_type=F32)


def _sigmoid(x):
    return 1.0 / (1.0 + jnp.exp(-x))


def _rmsnorm(x, g):
    return x * lax.rsqrt(jnp.mean(x * x, axis=-1, keepdims=True) + EPS) * g


def _mixer_kernel(x_ref, g_ref, wt_ref, wrq_ref, wk_ref, wat_ref, wbt_ref, wout_ref, sinks_ref, gch_ref,
                  feat_ref, tok_ref, dmask_ref, zeta_ref, xi_ref,
                  o_ref,
                  h_scr, ht_scr, qpad_scr, kband_scr, vt_scr, oatt_scr,
                  rq_scr, rk_scr, rv_scr, silu_scr, rb_scr, state_scr, mixed_scr, sgb_scr, mixedb_scr):
    tm = x_ref.shape[1]
    t = pl.program_id(1)

    @pl.when(t == 0)
    def _():
        kband_scr[0:ATT_BLOCK, :] = jnp.zeros((ATT_BLOCK, LANES), BF16)
        vt_scr[:, 0:ATT_BLOCK] = jnp.zeros((ATT_KV, ATT_BLOCK), BF16)
        qpad_scr[...] = jnp.zeros(qpad_scr.shape, BF16)
        state_scr[...] = jnp.zeros(state_scr.shape, F32)

    x = x_ref[0]
    hf = _rmsnorm(x, g_ref[...])
    h_scr[...] = hf.astype(BF16)
    ht_scr[...] = hf.T.astype(BF16)
    h = h_scr[...]
    ht = ht_scr[...]

    zk = _dot(h, wk_ref[...])

    def proj(r0, nrows):
        return _dot(wt_ref[r0:r0 + nrows, :], ht)

    kscale = RET_QK_DIM ** -0.5
    hq = RET_QK_DIM // 2

    def item_rk(hd):
        kk = zk[:, ATT_KV + RET_QK_DIM * hd:ATT_KV + RET_QK_DIM * (hd + 1)]
        rk_scr[:, RET_QK_DIM * hd:RET_QK_DIM * (hd + 1)] = (
            (kk * tok_ref[:, TOK_CKR:TOK_CKR + LANES]
             + pltpu.roll(kk, LANES // 2, 1) * tok_ref[:, TOK_SKR:TOK_SKR + LANES]) * kscale).astype(BF16)

    def item_rq(i):
        blk = _dot(wrq_ref[i * ROW_BLOCK:(i + 1) * ROW_BLOCK, :], ht)
        cqr = feat_ref[FEAT_CQR:FEAT_SQR, :]
        sqr = feat_ref[FEAT_SQR:FEAT_END, :]
        for hl in range(ROW_BLOCK // RET_QK_DIM):
            base = i * ROW_BLOCK + hl * RET_QK_DIM
            ev = blk[hl * RET_QK_DIM:hl * RET_QK_DIM + hq]
            od = blk[hl * RET_QK_DIM + hq:(hl + 1) * RET_QK_DIM]
            rq_scr[base:base + hq, :] = (ev * cqr - od * sqr).astype(BF16)
            rq_scr[base + hq:base + RET_QK_DIM, :] = (od * cqr + ev * sqr).astype(BF16)

    def item_rv(i):
        rv_scr[i * ROW_BLOCK:(i + 1) * ROW_BLOCK, :] = proj(OFF_RV + i * ROW_BLOCK, ROW_BLOCK)

    def item_rg(i):
        rg = proj(OFF_RG + i * ROW_BLOCK, ROW_BLOCK)
        silu_scr[i * ROW_BLOCK:(i + 1) * ROW_BLOCK, :] = rg * _sigmoid(rg)

    def item_ga(i):
        ga = proj(OFF_GA + i * ROW_BLOCK, ROW_BLOCK)
        ya = _dot(wat_ref[i * ROW_BLOCK:(i + 1) * ROW_BLOCK, :], oatt_scr[...])
        mixed_scr[i * ROW_BLOCK:(i + 1) * ROW_BLOCK, :] = _sigmoid(ga) * ya

    def item_gb(i):
        sgb_scr[i * ROW_BLOCK:(i + 1) * ROW_BLOCK, :] = _sigmoid(proj(OFF_GB + i * ROW_BLOCK, ROW_BLOCK))

    def spread(items, n_units):
        cuts = [round(u * len(items) / n_units) for u in range(n_units + 1)]
        return [items[cuts[u]:cuts[u + 1]] for u in range(n_units)]

    scale = HEAD_DIM ** -0.5

    def item_aq(j):
        qt = proj(OFF_AQ + j * GQA_GROUP * HEAD_DIM, GQA_GROUP * HEAD_DIM)
        cq = feat_ref[FEAT_CQA:FEAT_SQA, :]
        sq = feat_ref[FEAT_SQA:FEAT_CQR, :]
        for g in range(GQA_GROUP):
            x1 = qt[HEAD_DIM * g:HEAD_DIM * g + HALF]
            x2 = qt[HEAD_DIM * g + HALF:HEAD_DIM * (g + 1)]
            qpad_scr[j, HALF * j:HALF * (j + 1), g * tm:(g + 1) * tm] = (
                (x1 * cq - x2 * sq) * scale).astype(BF16)
            qpad_scr[j, 2 * HALF + HALF * j:2 * HALF + HALF * (j + 1), g * tm:(g + 1) * tm] = (
                (x2 * cq + x1 * sq) * scale).astype(BF16)

    nblk = D_MODEL // ROW_BLOCK
    att_items = ([functools.partial(item_aq, j) for j in range(1, N_KV_HEADS)]
                 + [functools.partial(item_rk, hd) for hd in range(RET_HEADS)]
                 + [functools.partial(item_rq, i) for i in range(RET_QK // ROW_BLOCK)]
                 + [functools.partial(item_rv, i) for i in range(RET_V // ROW_BLOCK)]
                 + [functools.partial(item_rg, i) for i in range(RET_V // ROW_BLOCK)])
    ret_items = ([functools.partial(item_ga, i) for i in range(nblk)]
                 + [functools.partial(item_gb, i) for i in range(nblk)])

    vt_scr[:, ATT_BLOCK:ATT_BLOCK + tm] = proj(OFF_AV, ATT_KV).astype(BF16)
    ka = zk[:, 0:ATT_KV]
    kband_scr[ATT_BLOCK:ATT_BLOCK + tm, :] = (
        ka * tok_ref[:, TOK_CKA:TOK_CKA + LANES]
        + pltpu.roll(ka, LANES // 2, 1) * tok_ref[:, TOK_SKA:TOK_SKA + LANES]).astype(BF16)
    item_aq(0)

    ncol = GQA_GROUP * ATT_BLOCK
    kj = lax.broadcasted_iota(jnp.int32, (2 * ATT_BLOCK, ncol), 0)
    qi = lax.broadcasted_iota(jnp.int32, (2 * ATT_BLOCK, ncol), 1) % ATT_BLOCK
    band = (kj > qi) & (kj <= qi + WINDOW)
    band_first = band & ((kj >= ATT_BLOCK) | (t > 0))
    sink_rows = [
        jnp.concatenate([jnp.full((1, ATT_BLOCK), sinks_ref[GQA_GROUP * j + g], F32)
                         for g in range(GQA_GROUP)], axis=1) for j in range(N_KV_HEADS)]

    def att_scores(c, j):
        r0 = c * ATT_BLOCK
        qs = jnp.concatenate(
            [qpad_scr[j, :, g * tm + r0:g * tm + r0 + ATT_BLOCK] for g in range(GQA_GROUP)],
            axis=1)
        kb = kband_scr[r0:r0 + 2 * ATT_BLOCK, :]
        return jnp.where(band_first if c == 0 else band, _dot(kb, qs), -jnp.inf)

    def att_finish(c, j, st):
        r0 = c * ATT_BLOCK
        m = jnp.maximum(jnp.max(st, axis=0, keepdims=True), sink_rows[j])
        p = jnp.exp(st - m)
        denom = jnp.sum(p, axis=0, keepdims=True) + jnp.exp(sink_rows[j] - m)
        vb = vt_scr[HEAD_DIM * j:HEAD_DIM * (j + 1), r0:r0 + 2 * ATT_BLOCK]
        ot = _dot(vb, p.astype(BF16)) / denom
        for g in range(GQA_GROUP):
            hh = GQA_GROUP * j + g
            oatt_scr[HEAD_DIM * hh:HEAD_DIM * (hh + 1), r0:r0 + ATT_BLOCK] = (
                ot[:, g * ATT_BLOCK:(g + 1) * ATT_BLOCK].astype(BF16))

    att_units = [(c, j) for j in range(N_KV_HEADS) for c in range(tm // ATT_BLOCK)]
    att_sched = spread(att_items, len(att_units))
    st_next = att_scores(*att_units[0])
    for u, unit in enumerate(att_units):
        st_cur = st_next
        if u + 1 < len(att_units):
            st_next = att_scores(*att_units[u + 1])
        for item in att_sched[u]:
            item()
        att_finish(*unit, st_cur)

    kband_scr[0:ATT_BLOCK, :] = kband_scr[tm:tm + ATT_BLOCK, :]
    vt_scr[:, 0:ATT_BLOCK] = vt_scr[:, tm:tm + ATT_BLOCK]

    def ret_scores(c, hd):
        r0 = c * RET_BLOCK
        q = rq_scr[RET_QK_DIM * hd:RET_QK_DIM * (hd + 1), r0:r0 + RET_BLOCK]
        kk = rk_scr[r0:r0 + RET_BLOCK, RET_QK_DIM * hd:RET_QK_DIM * (hd + 1)]
        return (_dot(kk, q) * dmask_ref[hd]).astype(BF16)

    def ret_finish(c, hd, s):
        r0 = c * RET_BLOCK
        q = rq_scr[RET_QK_DIM * hd:RET_QK_DIM * (hd + 1), r0:r0 + RET_BLOCK]
        kk = rk_scr[r0:r0 + RET_BLOCK, RET_QK_DIM * hd:RET_QK_DIM * (hd + 1)]
        vv = rv_scr[RET_V_DIM * hd:RET_V_DIM * (hd + 1), r0:r0 + RET_BLOCK]
        state = state_scr[hd]
        inner = _dot(vv.astype(BF16), s)
        cross = _dot(state.astype(BF16), q) * xi_ref[hd][0:1, :]
        out = inner + cross
        kv = _dot((vv * zeta_ref[hd][0:1, :]).astype(BF16), kk)
        state_scr[hd] = state * gch_ref[hd] + kv
        rn = out * lax.rsqrt(jnp.mean(out * out, axis=0, keepdims=True) + EPS)
        rb_scr[RET_V_DIM * hd:RET_V_DIM * (hd + 1), r0:r0 + RET_BLOCK] = (
            rn * silu_scr[RET_V_DIM * hd:RET_V_DIM * (hd + 1), r0:r0 + RET_BLOCK]).astype(BF16)

    ret_units = [(c, hd) for c in range(tm // RET_BLOCK) for hd in range(RET_HEADS)]
    ret_sched = spread(ret_items, len(ret_units))
    s_next = ret_scores(*ret_units[0])
    for u, unit in enumerate(ret_units):
        s_cur = s_next
        if u + 1 < len(ret_units):
            s_next = ret_scores(*ret_units[u + 1])
        for item in ret_sched[u]:
            item()
        ret_finish(*unit, s_cur)

    rb = rb_scr[...]
    for i in range(nblk):
        rows = slice(i * ROW_BLOCK, (i + 1) * ROW_BLOCK)
        yb = _dot(wbt_ref[rows, :], rb)
        mixedb_scr[rows, :] = (mixed_scr[rows, :] + sgb_scr[rows, :] * yb).astype(BF16)
    o_ref[0] = x_ref[0] + _dot_tn(mixedb_scr[...], wout_ref[...])


def _mlp_kernel(x_ref, g_ref, wup_ref, wdown_ref, gfin_ref, o_ref, *, final_norm):
    x = x_ref[...]
    h = _rmsnorm(x, g_ref[...]).astype(BF16)
    acc = jnp.zeros(x.shape, F32)
    for j in range(D_FF // FF_CHUNK):
        u = _dot(h, wup_ref[:, j * FF_CHUNK:(j + 1) * FF_CHUNK])
        a = jnp.square(jnp.maximum(u, 0.0)).astype(BF16)
        acc = acc + _dot(a, wdown_ref[j * FF_CHUNK:(j + 1) * FF_CHUNK, :])
    y = x + acc
    if final_norm:
        y = _rmsnorm(y, gfin_ref[...])
    o_ref[...] = y


def _resident(shape):
    return pl.BlockSpec(shape, lambda *_: (0,) * len(shape), pipeline_mode=pl.Buffered(1))


def _resident_layer(stacked, layer):
    tail = stacked.shape[1:]
    return pl.BlockSpec((None,) + tail, lambda *_: (layer,) + (0,) * len(tail),
                        pipeline_mode=pl.Buffered(1))


def _mixer_call(x, layer, g, wt, wrq, wk, wat, wbt, wout, sinks, gch, tables):
    bsz, seq, d = x.shape
    tm = min(SEQ_TILE, seq)
    feat_tab, tok_tab, dmask, zeta, xi = tables
    tok_spec = pl.BlockSpec((tm, tok_tab.shape[1]), lambda b, t: (t, 0))
    feat_spec = pl.BlockSpec((feat_tab.shape[0], tm), lambda b, t: (0, t))
    smem = pl.BlockSpec(memory_space=pltpu.SMEM)
    x_spec = pl.BlockSpec((1, tm, d), lambda b, t: (b, t, 0))
    return pl.pallas_call(
        _mixer_kernel,
        grid=(bsz, seq // tm),
        in_specs=[x_spec] + [_resident_layer(w, layer) for w in (g, wt, wrq, wk, wat, wbt, wout)]
        + [smem, smem, feat_spec, tok_spec,
           _resident(dmask.shape), _resident(zeta.shape), _resident(xi.shape)],
        out_specs=x_spec,
        out_shape=jax.ShapeDtypeStruct(x.shape, F32),
        scratch_shapes=[
            pltpu.VMEM((tm, d), BF16),
            pltpu.VMEM((d, tm), BF16),
            pltpu.VMEM((N_KV_HEADS, LANES, GQA_GROUP * tm), BF16),
            pltpu.VMEM((tm + ATT_BLOCK, LANES), BF16),
            pltpu.VMEM((ATT_KV, tm + ATT_BLOCK), BF16),
            pltpu.VMEM((ATT_Q, tm), BF16),
            pltpu.VMEM((RET_QK, tm), BF16),
            pltpu.VMEM((tm, RET_QK), BF16),
            pltpu.VMEM((RET_V, tm), F32),
            pltpu.VMEM((RET_V, tm), F32),
            pltpu.VMEM((RET_V, tm), BF16),
            pltpu.VMEM((RET_HEADS, RET_V_DIM, RET_QK_DIM), F32),
            pltpu.VMEM((d, tm), F32),
            pltpu.VMEM((d, tm), F32),
            pltpu.VMEM((d, tm), BF16),
        ],
        compiler_params=pltpu.CompilerParams(
            dimension_semantics=("arbitrary", "arbitrary"),
            vmem_limit_bytes=VMEM_LIMIT_BYTES),
        name="mixer",
    )(x, g, wt, wrq, wk, wat, wbt, wout, sinks, gch, feat_tab, tok_tab, dmask, zeta, xi)


def _mlp_call(x2d, layer, g, wup, wdown, gfin, final_norm):
    n, d = x2d.shape
    tm = min(MLP_TILE, n)
    x_spec = pl.BlockSpec((tm, d), lambda i: (i, 0))
    return pl.pallas_call(
        functools.partial(_mlp_kernel, final_norm=final_norm),
        grid=(n // tm,),
        in_specs=[x_spec] + [_resident_layer(w, layer) for w in (g, wup, wdown)]
        + [_resident((1, d))],
        out_specs=x_spec,
        out_shape=jax.ShapeDtypeStruct(x2d.shape, F32),
        compiler_params=pltpu.CompilerParams(
            dimension_semantics=("arbitrary",),
            vmem_limit_bytes=VMEM_LIMIT_BYTES),
        name="mlp_final" if final_norm else "mlp",
    )(x2d, g, wup, wdown, gfin)


_AK_PERM = np.concatenate([np.arange(0, HALF), np.arange(2 * HALF, 3 * HALF),
                           np.arange(HALF, 2 * HALF), np.arange(3 * HALF, 4 * HALF)])
_RET_PERM = np.concatenate([
    np.concatenate([h * RET_QK_DIM + np.arange(0, RET_QK_DIM, 2),
                    h * RET_QK_DIM + np.arange(1, RET_QK_DIM, 2)]) for h in range(RET_HEADS)])


def _prepare_in_projection(w_in):
    wt = jnp.swapaxes(w_in, 1, 2).astype(BF16)
    wrq = jnp.swapaxes(w_in[:, :, OFF_RQ:OFF_RK][:, :, _RET_PERM], 1, 2).astype(BF16)
    wk = jnp.concatenate([w_in[:, :, OFF_AK:OFF_AV][:, :, _AK_PERM],
                          w_in[:, :, OFF_RK:OFF_RV][:, :, _RET_PERM]], axis=2).astype(BF16)
    return wt, wrq, wk


def _position_tables(seq):
    pos = jnp.arange(seq, dtype=F32)
    inv = ROPE_THETA ** (-jnp.arange(HALF, dtype=F32) / HALF)
    ang = pos[:, None] * inv[None, :]
    cos, sin = jnp.cos(ang), jnp.sin(ang)
    theta = 1.0 / (RET_THETA ** jnp.linspace(0.0, 1.0, RET_QK_DIM // 2, dtype=F32))
    angr = pos[:, None] * theta[None, :]
    cosr, sinr = jnp.cos(angr), jnp.sin(angr)
    tok_tab = jnp.concatenate([cos, cos, cos, cos, -sin, -sin, sin, sin,
                               cosr, cosr, -sinr, sinr], axis=1)
    feat_tab = jnp.concatenate([cos, sin, cosr, sinr], axis=1).T

    log_g = jnp.log(1.0 - 2.0 ** (-5.0 - jnp.arange(RET_HEADS, dtype=F32)))
    idx = jnp.arange(RET_BLOCK, dtype=F32)
    rel = idx[None, :] - idx[:, None]
    dmask = jnp.where(rel[None] >= 0,
                      jnp.exp(log_g[:, None, None] * jnp.maximum(rel, 0.0)[None]), 0.0)
    zeta = jnp.exp(log_g[:, None] * (RET_BLOCK - 1.0 - idx)[None, :])
    xi = jnp.exp(log_g[:, None] * (idx + 1.0)[None, :])
    zeta = jnp.broadcast_to(zeta[:, None, :], (RET_HEADS, 8, RET_BLOCK))
    xi = jnp.broadcast_to(xi[:, None, :], (RET_HEADS, 8, RET_BLOCK))
    gch = jnp.exp(log_g * RET_BLOCK)
    return (feat_tab, tok_tab, dmask, zeta, xi), gch


def kernel(x, g_mix, w_in, sinks, w_a, w_b, w_out, g_mlp, w_up, w_down, g_final):
    bsz, seq, d = x.shape
    depth = w_in.shape[0]
    tables, gch = _position_tables(seq)
    wt, wrq, wk = _prepare_in_projection(w_in)
    wat = jnp.swapaxes(w_a, 1, 2).astype(BF16)
    wbt = jnp.swapaxes(w_b, 1, 2).astype(BF16)
    wout = w_out.astype(BF16)
    wup = w_up.astype(BF16)
    wdown = w_down.astype(BF16)
    g_mix3 = g_mix[:, None, :]
    g_mlp3 = g_mlp[:, None, :]
    for l in range(depth):
        x = _mixer_call(x, l, g_mix3, wt, wrq, wk, wat, wbt, wout, sinks[l], gch, tables)
        x = _mlp_call(x.reshape(bsz * seq, d), l, g_mlp3, wup, wdown, g_final[None, :],
                      final_norm=(l == depth - 1)).reshape(bsz, seq, d)
    return x
```

```python
import functools

import jax
import jax.numpy as jnp
from jax import lax
from jax.experimental import pallas as pl
from jax.experimental.pallas import tpu as pltpu

D_MODEL = 1024
HEAD_DIM = 64
N_Q_HEADS = 8
N_KV_HEADS = 2
GQA_GROUP = N_Q_HEADS // N_KV_HEADS
WINDOW = 128
ROPE_THETA = 10000.0
RET_HEADS = 4
RET_QK_DIM = 128
RET_V_DIM = 2 * RET_QK_DIM
RET_THETA = 10000.0
D_FF = 4 * D_MODEL
EPS = 1e-6

ATT_Q = N_Q_HEADS * HEAD_DIM
ATT_KV = N_KV_HEADS * HEAD_DIM
RET_QK = RET_HEADS * RET_QK_DIM
RET_V = RET_HEADS * RET_V_DIM

OFF_AQ = 0
OFF_AK = OFF_AQ + ATT_Q
OFF_AV = OFF_AK + ATT_KV
OFF_RQ = OFF_AV + ATT_KV
OFF_RK = OFF_RQ + RET_QK
OFF_RV = OFF_RK + RET_QK
OFF_RG = OFF_RV + RET_V
OFF_GA = OFF_RG + RET_V
OFF_GB = OFF_GA + D_MODEL
W_IN = OFF_GB + D_MODEL

TOK_CKA, TOK_SKA, TOK_CKR, TOK_SKR = 0, 128, 256, 384
FEAT_CQA = 0
FEAT_SQA = FEAT_CQA + HEAD_DIM // 2
FEAT_CQR = FEAT_SQA + HEAD_DIM // 2
FEAT_SQR = FEAT_CQR + RET_QK_DIM // 2
FEAT_END = FEAT_SQR + RET_QK_DIM // 2

LANES = 128
HALF = HEAD_DIM // 2
ATT_BLOCK = WINDOW
RET_BLOCK = 256
SEQ_TILE = 512
ROW_BLOCK = 256
MLP_TILE = 1024
FF_CHUNK = 1024
VMEM_LIMIT_BYTES = 56 * 1024 * 1024

BF16 = jnp.bfloat16
F32 = jnp.float32


def _dot(a, b):
    return jnp.dot(a, b, preferred_element_type=F32)


def _dot_nt(a, b):
    return lax.dot_general(a, b, (((1,), (1,)), ((), ())), preferred_element_type=F32)


def _dot_tn(a, b):
    return lax.dot_general(a, b, (((0,), (0,)), ((), ())), preferred_element_type=F32)


def _sigmoid(x):
    return 1.0 / (1.0 + jnp.exp(-x))


def _rmsnorm(x, g):
    return x * lax.rsqrt(jnp.mean(x * x, axis=-1, keepdims=True) + EPS) * g


def _mixer_kernel(x_ref, xnext_ref, g_ref, wt_ref, wrq_ref, wk_ref, wat_ref, wbt_ref, wout_ref,
                  sinks_ref, gch_ref, feat_ref, tok_ref, dmask_ref, zeta_ref, xi_ref,
                  o_ref,
                  h_scr, ht_scr, qpad_scr, kband_scr, vt_scr, oatt_scr,
                  rq_scr, rk_scr, rv_scr, silu_scr, rb_scr, state_scr, mixed_scr, sgb_scr, mixedb_scr):
    tm = x_ref.shape[1]
    t = pl.program_id(1)

    @pl.when(t == 0)
    def _():
        kband_scr[0:ATT_BLOCK, :] = jnp.zeros((ATT_BLOCK, LANES), BF16)
        vt_scr[:, 0:ATT_BLOCK] = jnp.zeros((ATT_KV, ATT_BLOCK), BF16)
        qpad_scr[...] = jnp.zeros(qpad_scr.shape, BF16)
        state_scr[...] = jnp.zeros(state_scr.shape, F32)

    def norm_to_scratch(xv):
        hb = _rmsnorm(xv, g_ref[...]).astype(BF16)
        h_scr[...] = hb
        ht_scr[...] = hb.T

    @pl.when((pl.program_id(0) == 0) & (t == 0))
    def _():
        norm_to_scratch(x_ref[0])

    zk = _dot(h_scr[...], wk_ref[...])

    def proj(r0, nrows):
        return _dot(wt_ref[r0:r0 + nrows, :], ht_scr[...])

    kscale = RET_QK_DIM ** -0.5
    hq = RET_QK_DIM // 2

    def item_rk(hd):
        kk = zk[:, ATT_KV + RET_QK_DIM * hd:ATT_KV + RET_QK_DIM * (hd + 1)]
        rk_scr[:, RET_QK_DIM * hd:RET_QK_DIM * (hd + 1)] = (
            (kk * tok_ref[:, TOK_CKR:TOK_CKR + LANES]
             + pltpu.roll(kk, LANES // 2, 1) * tok_ref[:, TOK_SKR:TOK_SKR + LANES]) * kscale).astype(BF16)

    def item_rq(i):
        blk = _dot(wrq_ref[i * ROW_BLOCK:(i + 1) * ROW_BLOCK, :], ht_scr[...])
        cqr = feat_ref[FEAT_CQR:FEAT_SQR, :]
        sqr = feat_ref[FEAT_SQR:FEAT_END, :]
        for hl in range(ROW_BLOCK // RET_QK_DIM):
            base = i * ROW_BLOCK + hl * RET_QK_DIM
            ev = blk[hl * RET_QK_DIM:hl * RET_QK_DIM + hq]
            od = blk[hl * RET_QK_DIM + hq:(hl + 1) * RET_QK_DIM]
            rq_scr[base:base + hq, :] = (ev * cqr - od * sqr).astype(BF16)
            rq_scr[base + hq:base + RET_QK_DIM, :] = (od * cqr + ev * sqr).astype(BF16)

    def item_rv(i):
        rv_scr[i * ROW_BLOCK:(i + 1) * ROW_BLOCK, :] = proj(OFF_RV + i * ROW_BLOCK, ROW_BLOCK)

    def item_rg(i):
        rg = proj(OFF_RG + i * ROW_BLOCK, ROW_BLOCK)
        silu_scr[i * ROW_BLOCK:(i + 1) * ROW_BLOCK, :] = rg * _sigmoid(rg)

    def item_ga(i):
        ga = proj(OFF_GA + i * ROW_BLOCK, ROW_BLOCK)
        ya = _dot(wat_ref[i * ROW_BLOCK:(i + 1) * ROW_BLOCK, :], oatt_scr[...])
        mixed_scr[i * ROW_BLOCK:(i + 1) * ROW_BLOCK, :] = _sigmoid(ga) * ya

    def item_gb(i):
        sgb_scr[i * ROW_BLOCK:(i + 1) * ROW_BLOCK, :] = _sigmoid(proj(OFF_GB + i * ROW_BLOCK, ROW_BLOCK))

    def spread(items, n_units):
        cuts = [round(u * len(items) / n_units) for u in range(n_units + 1)]
        return [items[cuts[u]:cuts[u + 1]] for u in range(n_units)]

    scale = HEAD_DIM ** -0.5

    def item_aq(j):
        qt = proj(OFF_AQ + j * GQA_GROUP * HEAD_DIM, GQA_GROUP * HEAD_DIM)
        cq = feat_ref[FEAT_CQA:FEAT_SQA, :]
        sq = feat_ref[FEAT_SQA:FEAT_CQR, :]
        for g in range(GQA_GROUP):
            x1 = qt[HEAD_DIM * g:HEAD_DIM * g + HALF]
            x2 = qt[HEAD_DIM * g + HALF:HEAD_DIM * (g + 1)]
            qpad_scr[j, HALF * j:HALF * (j + 1), g * tm:(g + 1) * tm] = (
                (x1 * cq - x2 * sq) * scale).astype(BF16)
            qpad_scr[j, 2 * HALF + HALF * j:2 * HALF + HALF * (j + 1), g * tm:(g + 1) * tm] = (
                (x2 * cq + x1 * sq) * scale).astype(BF16)

    nblk = D_MODEL // ROW_BLOCK
    att_items = ([functools.partial(item_aq, j) for j in range(1, N_KV_HEADS)]
                 + [functools.partial(item_rk, hd) for hd in range(RET_HEADS)]
                 + [functools.partial(item_rq, i) for i in range(RET_QK // ROW_BLOCK)]
                 + [functools.partial(item_rv, i) for i in range(RET_V // ROW_BLOCK)]
                 + [functools.partial(item_rg, i) for i in range(RET_V // ROW_BLOCK)])
    ret_items = ([functools.partial(item_ga, i) for i in range(nblk)]
                 + [functools.partial(item_gb, i) for i in range(nblk)])

    vt_scr[:, ATT_BLOCK:ATT_BLOCK + tm] = proj(OFF_AV, ATT_KV).astype(BF16)
    ka = zk[:, 0:ATT_KV]
    kband_scr[ATT_BLOCK:ATT_BLOCK + tm, :] = (
        ka * tok_ref[:, TOK_CKA:TOK_CKA + LANES]
        + pltpu.roll(ka, LANES // 2, 1) * tok_ref[:, TOK_SKA:TOK_SKA + LANES]).astype(BF16)
    item_aq(0)

    ncol = GQA_GROUP * ATT_BLOCK
    kj = lax.broadcasted_iota(jnp.int32, (2 * ATT_BLOCK, ncol), 0)
    qi = lax.broadcasted_iota(jnp.int32, (2 * ATT_BLOCK, ncol), 1) % ATT_BLOCK
    band = (kj > qi) & (kj <= qi + WINDOW)
    band_first = band & ((kj >= ATT_BLOCK) | (t > 0))
    sink_rows = [
        jnp.concatenate([jnp.full((1, ATT_BLOCK), sinks_ref[GQA_GROUP * j + g], F32)
                         for g in range(GQA_GROUP)], axis=1) for j in range(N_KV_HEADS)]

    def att_scores(c, j):
        r0 = c * ATT_BLOCK
        qs = jnp.concatenate(
            [qpad_scr[j, :, g * tm + r0:g * tm + r0 + ATT_BLOCK] for g in range(GQA_GROUP)],
            axis=1)
        kb = kband_scr[r0:r0 + 2 * ATT_BLOCK, :]
        return jnp.where(band_first if c == 0 else band, _dot(kb, qs), -jnp.inf)

    def att_finish(c, j, st):
        r0 = c * ATT_BLOCK
        m = jnp.maximum(jnp.max(st, axis=0, keepdims=True), sink_rows[j])
        p = jnp.exp(st - m)
        denom = jnp.sum(p, axis=0, keepdims=True) + jnp.exp(sink_rows[j] - m)
        vb = vt_scr[HEAD_DIM * j:HEAD_DIM * (j + 1), r0:r0 + 2 * ATT_BLOCK]
        ot = _dot(vb, p.astype(BF16)) / denom
        for g in range(GQA_GROUP):
            hh = GQA_GROUP * j + g
            oatt_scr[HEAD_DIM * hh:HEAD_DIM * (hh + 1), r0:r0 + ATT_BLOCK] = (
                ot[:, g * ATT_BLOCK:(g + 1) * ATT_BLOCK].astype(BF16))

    att_units = [(c, j) for j in range(N_KV_HEADS) for c in range(tm // ATT_BLOCK)]
    att_sched = spread(att_items, len(att_units))
    st_next = att_scores(*att_units[0])
    for u, unit in enumerate(att_units):
        st_cur = st_next
        if u + 1 < len(att_units):
            st_next = att_scores(*att_units[u + 1])
        for item in att_sched[u]:
            item()
        att_finish(*unit, st_cur)

    kband_scr[0:ATT_BLOCK, :] = kband_scr[tm:tm + ATT_BLOCK, :]
    vt_scr[:, 0:ATT_BLOCK] = vt_scr[:, tm:tm + ATT_BLOCK]

    def ret_scores(c, hd):
        r0 = c * RET_BLOCK
        q = rq_scr[RET_QK_DIM * hd:RET_QK_DIM * (hd + 1), r0:r0 + RET_BLOCK]
        kk = rk_scr[r0:r0 + RET_BLOCK, RET_QK_DIM * hd:RET_QK_DIM * (hd + 1)]
        return (_dot(kk, q) * dmask_ref[hd]).astype(BF16)

    def ret_finish(c, hd, s):
        r0 = c * RET_BLOCK
        q = rq_scr[RET_QK_DIM * hd:RET_QK_DIM * (hd + 1), r0:r0 + RET_BLOCK]
        kk = rk_scr[r0:r0 + RET_BLOCK, RET_QK_DIM * hd:RET_QK_DIM * (hd + 1)]
        vv = rv_scr[RET_V_DIM * hd:RET_V_DIM * (hd + 1), r0:r0 + RET_BLOCK]
        state = state_scr[hd]
        inner = _dot(vv.astype(BF16), s)
        cross = _dot(state.astype(BF16), q) * xi_ref[hd][0:1, :]
        out = inner + cross
        kv = _dot((vv * zeta_ref[hd][0:1, :]).astype(BF16), kk)
        state_scr[hd] = state * gch_ref[hd] + kv
        rn = out * lax.rsqrt(jnp.mean(out * out, axis=0, keepdims=True) + EPS)
        rb_scr[RET_V_DIM * hd:RET_V_DIM * (hd + 1), r0:r0 + RET_BLOCK] = (
            rn * silu_scr[RET_V_DIM * hd:RET_V_DIM * (hd + 1), r0:r0 + RET_BLOCK]).astype(BF16)

    ret_units = [(c, hd) for c in range(tm // RET_BLOCK) for hd in range(RET_HEADS)]
    ret_sched = spread(ret_items, len(ret_units))
    s_next = ret_scores(*ret_units[0])
    for u, unit in enumerate(ret_units):
        s_cur = s_next
        if u + 1 < len(ret_units):
            s_next = ret_scores(*ret_units[u + 1])
        for item in ret_sched[u]:
            item()
        ret_finish(*unit, s_cur)

    norm_to_scratch(xnext_ref[0])
    rb = rb_scr[...]
    for i in range(nblk):
        rows = slice(i * ROW_BLOCK, (i + 1) * ROW_BLOCK)
        yb = _dot(wbt_ref[rows, :], rb)
        mixedb_scr[rows, :] = (mixed_scr[rows, :] + sgb_scr[rows, :] * yb).astype(BF16)
    o_ref[0] = x_ref[0] + _dot_tn(mixedb_scr[...], wout_ref[...])


def _mlp_kernel(x_ref, g_ref, wup_ref, wdown_ref, gfin_ref, o_ref, *, final_norm):
    x = x_ref[...]
    h = _rmsnorm(x, g_ref[...]).astype(BF16)
    acc = jnp.zeros(x.shape, F32)
    for j in range(D_FF // FF_CHUNK):
        u = _dot(h, wup_ref[:, j * FF_CHUNK:(j + 1) * FF_CHUNK])
        a = jnp.square(jnp.maximum(u, 0.0)).astype(BF16)
        acc = acc + _dot(a, wdown_ref[j * FF_CHUNK:(j + 1) * FF_CHUNK, :])
    y = x + acc
    if final_norm:
        y = _rmsnorm(y, gfin_ref[...])
    o_ref[...] = y


def _resident(shape):
    return pl.BlockSpec(shape, lambda *_: (0,) * len(shape), pipeline_mode=pl.Buffered(1))


def _resident_layer(stacked, layer):
    tail = stacked.shape[1:]
    return pl.BlockSpec((None,) + tail, lambda *_: (layer,) + (0,) * len(tail),
                        pipeline_mode=pl.Buffered(1))


def _mixer_call(x, layer, g, wt, wrq, wk, wat, wbt, wout, sinks, gch, tables):
    bsz, seq, d = x.shape
    tm = min(SEQ_TILE, seq)
    feat_tab, tok_tab, dmask, zeta, xi = tables
    tok_spec = pl.BlockSpec((tm, tok_tab.shape[1]), lambda b, t: (t, 0))
    feat_spec = pl.BlockSpec((feat_tab.shape[0], tm), lambda b, t: (0, t))
    smem = pl.BlockSpec(memory_space=pltpu.SMEM)
    x_spec = pl.BlockSpec((1, tm, d), lambda b, t: (b, t, 0))
    nt = seq // tm

    def next_tile(b, t):
        flat = jnp.minimum(b * nt + t + 1, bsz * nt - 1)
        return (flat // nt, flat % nt, 0)

    return pl.pallas_call(
        _mixer_kernel,
        grid=(bsz, nt),
        in_specs=[x_spec, pl.BlockSpec((1, tm, d), next_tile)]
        + [_resident_layer(w, layer) for w in (g, wt, wrq, wk, wat, wbt, wout)]
        + [smem, smem, feat_spec, tok_spec,
           _resident(dmask.shape), _resident(zeta.shape), _resident(xi.shape)],
        out_specs=x_spec,
        out_shape=jax.ShapeDtypeStruct(x.shape, F32),
        scratch_shapes=[
            pltpu.VMEM((tm, d), BF16),
            pltpu.VMEM((d, tm), BF16),
            pltpu.VMEM((N_KV_HEADS, LANES, GQA_GROUP * tm), BF16),
            pltpu.VMEM((tm + ATT_BLOCK, LANES), BF16),
            pltpu.VMEM((ATT_KV, tm + ATT_BLOCK), BF16),
            pltpu.VMEM((ATT_Q, tm), BF16),
            pltpu.VMEM((RET_QK, tm), BF16),
            pltpu.VMEM((tm, RET_QK), BF16),
            pltpu.VMEM((RET_V, tm), F32),
            pltpu.VMEM((RET_V, tm), F32),
            pltpu.VMEM((RET_V, tm), BF16),
            pltpu.VMEM((RET_HEADS, RET_V_DIM, RET_QK_DIM), F32),
            pltpu.VMEM((d, tm), F32),
            pltpu.VMEM((d, tm), F32),
            pltpu.VMEM((d, tm), BF16),
        ],
        compiler_params=pltpu.CompilerParams(
            dimension_semantics=("arbitrary", "arbitrary"),
            vmem_limit_bytes=VMEM_LIMIT_BYTES),
        name="mixer",
    )(x, x, g, wt, wrq, wk, wat, wbt, wout, sinks, gch, feat_tab, tok_tab, dmask, zeta, xi)


def _mlp_call(x2d, layer, g, wup, wdown, gfin, final_norm):
    n, d = x2d.shape
    tm = min(MLP_TILE, n)
    x_spec = pl.BlockSpec((tm, d), lambda i: (i, 0))
    return pl.pallas_call(
        functools.partial(_mlp_kernel, final_norm=final_norm),
        grid=(n // tm,),
        in_specs=[x_spec] + [_resident_layer(w, layer) for w in (g, wup, wdown)]
        + [_resident((1, d))],
        out_specs=x_spec,
        out_shape=jax.ShapeDtypeStruct(x2d.shape, F32),
        compiler_params=pltpu.CompilerParams(
            dimension_semantics=("arbitrary",),
            vmem_limit_bytes=VMEM_LIMIT_BYTES),
        name="mlp_final" if final_norm else "mlp",
    )(x2d, g, wup, wdown, gfin)


def _even_odd_heads(w):
    lead = w.shape[:-1]
    w = w.reshape(lead + (RET_HEADS, RET_QK_DIM // 2, 2))
    return jnp.swapaxes(w, -1, -2).reshape(lead + (RET_QK,))


def _halves_first(w):
    lead = w.shape[:-1]
    w = w.reshape(lead + (N_KV_HEADS, 2, HALF))
    return jnp.swapaxes(w, -2, -3).reshape(lead + (ATT_KV,))


def _prepare_in_projection(w_in):
    wt = jnp.swapaxes(w_in, 1, 2).astype(BF16)
    wrq = jnp.swapaxes(_even_odd_heads(w_in[:, :, OFF_RQ:OFF_RK]), 1, 2).astype(BF16)
    wk = jnp.concatenate([_halves_first(w_in[:, :, OFF_AK:OFF_AV]),
                          _even_odd_heads(w_in[:, :, OFF_RK:OFF_RV])], axis=2).astype(BF16)
    return wt, wrq, wk


def _position_tables(seq):
    pos = jnp.arange(seq, dtype=F32)
    inv = ROPE_THETA ** (-jnp.arange(HALF, dtype=F32) / HALF)
    ang = pos[:, None] * inv[None, :]
    cos, sin = jnp.cos(ang), jnp.sin(ang)
    theta = 1.0 / (RET_THETA ** jnp.linspace(0.0, 1.0, RET_QK_DIM // 2, dtype=F32))
    angr = pos[:, None] * theta[None, :]
    cosr, sinr = jnp.cos(angr), jnp.sin(angr)
    tok_tab = jnp.concatenate([cos, cos, cos, cos, -sin, -sin, sin, sin,
                               cosr, cosr, -sinr, sinr], axis=1)
    feat_tab = jnp.concatenate([cos, sin, cosr, sinr], axis=1).T

    log_g = jnp.log(1.0 - 2.0 ** (-5.0 - jnp.arange(RET_HEADS, dtype=F32)))
    idx = jnp.arange(RET_BLOCK, dtype=F32)
    rel = idx[None, :] - idx[:, None]
    dmask = jnp.where(rel[None] >= 0,
                      jnp.exp(log_g[:, None, None] * jnp.maximum(rel, 0.0)[None]), 0.0)
    zeta = jnp.exp(log_g[:, None] * (RET_BLOCK - 1.0 - idx)[None, :])
    xi = jnp.exp(log_g[:, None] * (idx + 1.0)[None, :])
    zeta = jnp.broadcast_to(zeta[:, None, :], (RET_HEADS, 8, RET_BLOCK))
    xi = jnp.broadcast_to(xi[:, None, :], (RET_HEADS, 8, RET_BLOCK))
    gch = jnp.exp(log_g * RET_BLOCK)
    return (feat_tab, tok_tab, dmask, zeta, xi), gch


def kernel(x, g_mix, w_in, sinks, w_a, w_b, w_out, g_mlp, w_up, w_down, g_final):
    bsz, seq, d = x.shape
    depth = w_in.shape[0]
    tables, gch = _position_tables(seq)
    wt, wrq, wk = _prepare_in_projection(w_in)
    wat = jnp.swapaxes(w_a, 1, 2).astype(BF16)
    wbt = jnp.swapaxes(w_b, 1, 2).astype(BF16)
    wout = w_out.astype(BF16)
    wup = w_up.astype(BF16)
    wdown = w_down.astype(BF16)
    g_mix3 = g_mix[:, None, :]
    g_mlp3 = g_mlp[:, None, :]
    for l in range(depth):
        x = _mixer_call(x, l, g_mix3, wt, wrq, wk, wat, wbt, wout, sinks[l], gch, tables)
        x = _mlp_call(x.reshape(bsz * seq, d), l, g_mlp3, wup, wdown, g_final[None, :],
                      final_norm=(l == depth - 1)).reshape(bsz, seq, d)
    return x
```

```python
import functools

import numpy as np
import jax
import jax.numpy as jnp
from jax import lax
from jax.experimental import pallas as pl
from jax.experimental.pallas import tpu as pltpu

D_MODEL = 1024
HEAD_DIM = 64
N_Q_HEADS = 8
N_KV_HEADS = 2
GQA_GROUP = N_Q_HEADS // N_KV_HEADS
WINDOW = 128
ROPE_THETA = 10000.0
RET_HEADS = 4
RET_QK_DIM = 128
RET_V_DIM = 2 * RET_QK_DIM
RET_THETA = 10000.0
D_FF = 4 * D_MODEL
EPS = 1e-6

ATT_Q = N_Q_HEADS * HEAD_DIM
ATT_KV = N_KV_HEADS * HEAD_DIM
RET_QK = RET_HEADS * RET_QK_DIM
RET_V = RET_HEADS * RET_V_DIM

OFF_AQ = 0
OFF_AK = OFF_AQ + ATT_Q
OFF_AV = OFF_AK + ATT_KV
OFF_RQ = OFF_AV + ATT_KV
OFF_RK = OFF_RQ + RET_QK
OFF_RV = OFF_RK + RET_QK
OFF_RG = OFF_RV + RET_V
OFF_GA = OFF_RG + RET_V
OFF_GB = OFF_GA + D_MODEL
W_IN = OFF_GB + D_MODEL

TOK_CKA, TOK_SKA, TOK_CKR, TOK_SKR = 0, 128, 256, 384
FEAT_CQA = 0
FEAT_SQA = FEAT_CQA + HEAD_DIM // 2
FEAT_CQR = FEAT_SQA + HEAD_DIM // 2
FEAT_SQR = FEAT_CQR + RET_QK_DIM // 2
FEAT_END = FEAT_SQR + RET_QK_DIM // 2

LANES = 128
HALF = HEAD_DIM // 2
ATT_BLOCK = WINDOW
RET_BLOCK = 256
SEQ_TILE = 512
ROW_BLOCK = 256
MLP_TILE = 1024
FF_CHUNK = 1024
VMEM_LIMIT_BYTES = 56 * 1024 * 1024

BF16 = jnp.bfloat16
F32 = jnp.float32


def _dot(a, b):
    return jnp.dot(a, b, preferred_element_type=F32)


def _dot_nt(a, b):
    return lax.dot_general(a, b, (((1,), (1,)), ((), ())), preferred_element_type=F32)


def _dot_tn(a, b):
    return lax.dot_general(a, b, (((0,), (0,)), ((), ())), preferred_element_type=F32)


def _sigmoid(x):
    return 1.0 / (1.0 + jnp.exp(-x))


def _rmsnorm(x, g):
    return x * lax.rsqrt(jnp.mean(x * x, axis=-1, keepdims=True) + EPS) * g


def _mixer_kernel(x_ref, xnext_ref, g_ref, wt_ref, wrq_ref, wk_ref, wat_ref, wbt_ref, wout_ref,
                  sinks_ref, gch_ref, feat_ref, tok_ref, dmask_ref, zeta_ref, xi_ref,
                  o_ref,
                  h_scr, ht_scr, qpad_scr, kband_scr, vt_scr, oatt_scr,
                  rq_scr, rk_scr, rv_scr, silu_scr, rb_scr, state_scr, mixed_scr, sgb_scr, mixedb_scr):
    tm = x_ref.shape[1]
    t = pl.program_id(1)

    @pl.when(t == 0)
    def _():
        kband_scr[0:ATT_BLOCK, :] = jnp.zeros((ATT_BLOCK, LANES), BF16)
        vt_scr[:, 0:ATT_BLOCK] = jnp.zeros((ATT_KV, ATT_BLOCK), BF16)
        qpad_scr[...] = jnp.zeros(qpad_scr.shape, BF16)
        state_scr[...] = jnp.zeros(state_scr.shape, F32)

    def norm_to_scratch(xv):
        hb = _rmsnorm(xv, g_ref[...]).astype(BF16)
        h_scr[...] = hb
        ht_scr[...] = hb.T

    @pl.when((pl.program_id(0) == 0) & (t == 0))
    def _():
        norm_to_scratch(x_ref[0])

    zk = _dot(h_scr[...], wk_ref[...])

    def proj(r0, nrows):
        return _dot(wt_ref[r0:r0 + nrows, :], ht_scr[...])

    kscale = RET_QK_DIM ** -0.5
    hq = RET_QK_DIM // 2

    def item_rk(hd):
        kk = zk[:, ATT_KV + RET_QK_DIM * hd:ATT_KV + RET_QK_DIM * (hd + 1)]
        rk_scr[:, RET_QK_DIM * hd:RET_QK_DIM * (hd + 1)] = (
            (kk * tok_ref[:, TOK_CKR:TOK_CKR + LANES]
             + pltpu.roll(kk, LANES // 2, 1) * tok_ref[:, TOK_SKR:TOK_SKR + LANES]) * kscale).astype(BF16)

    def item_rq(i):
        blk = _dot(wrq_ref[i * ROW_BLOCK:(i + 1) * ROW_BLOCK, :], ht_scr[...])
        cqr = feat_ref[FEAT_CQR:FEAT_SQR, :]
        sqr = feat_ref[FEAT_SQR:FEAT_END, :]
        for hl in range(ROW_BLOCK // RET_QK_DIM):
            base = i * ROW_BLOCK + hl * RET_QK_DIM
            ev = blk[hl * RET_QK_DIM:hl * RET_QK_DIM + hq]
            od = blk[hl * RET_QK_DIM + hq:(hl + 1) * RET_QK_DIM]
            rq_scr[base:base + hq, :] = (ev * cqr - od * sqr).astype(BF16)
            rq_scr[base + hq:base + RET_QK_DIM, :] = (od * cqr + ev * sqr).astype(BF16)

    def item_rv(i):
        rv_scr[i * ROW_BLOCK:(i + 1) * ROW_BLOCK, :] = proj(OFF_RV + i * ROW_BLOCK, ROW_BLOCK)

    def item_rg(i):
        rg = proj(OFF_RG + i * ROW_BLOCK, ROW_BLOCK)
        silu_scr[i * ROW_BLOCK:(i + 1) * ROW_BLOCK, :] = rg * _sigmoid(rg)

    def item_ga(i):
        ga = proj(OFF_GA + i * ROW_BLOCK, ROW_BLOCK)
        ya = _dot(wat_ref[i * ROW_BLOCK:(i + 1) * ROW_BLOCK, :], oatt_scr[...])
        mixed_scr[i * ROW_BLOCK:(i + 1) * ROW_BLOCK, :] = _sigmoid(ga) * ya

    def item_gb(i):
        sgb_scr[i * ROW_BLOCK:(i + 1) * ROW_BLOCK, :] = _sigmoid(proj(OFF_GB + i * ROW_BLOCK, ROW_BLOCK))

    def spread(items, n_units):
        cuts = [round(u * len(items) / n_units) for u in range(n_units + 1)]
        return [items[cuts[u]:cuts[u + 1]] for u in range(n_units)]

    scale = HEAD_DIM ** -0.5

    def item_aq(j):
        qt = proj(OFF_AQ + j * GQA_GROUP * HEAD_DIM, GQA_GROUP * HEAD_DIM)
        cq = feat_ref[FEAT_CQA:FEAT_SQA, :]
        sq = feat_ref[FEAT_SQA:FEAT_CQR, :]
        for g in range(GQA_GROUP):
            x1 = qt[HEAD_DIM * g:HEAD_DIM * g + HALF]
            x2 = qt[HEAD_DIM * g + HALF:HEAD_DIM * (g + 1)]
            qpad_scr[j, HALF * j:HALF * (j + 1), g * tm:(g + 1) * tm] = (
                (x1 * cq - x2 * sq) * scale).astype(BF16)
            qpad_scr[j, 2 * HALF + HALF * j:2 * HALF + HALF * (j + 1), g * tm:(g + 1) * tm] = (
                (x2 * cq + x1 * sq) * scale).astype(BF16)

    nblk = D_MODEL // ROW_BLOCK
    att_items = ([functools.partial(item_aq, j) for j in range(1, N_KV_HEADS)]
                 + [functools.partial(item_rk, hd) for hd in range(RET_HEADS)]
                 + [functools.partial(item_rq, i) for i in range(RET_QK // ROW_BLOCK)]
                 + [functools.partial(item_rv, i) for i in range(RET_V // ROW_BLOCK)]
                 + [functools.partial(item_rg, i) for i in range(RET_V // ROW_BLOCK)])
    ret_items = ([functools.partial(item_ga, i) for i in range(nblk)]
                 + [functools.partial(item_gb, i) for i in range(nblk)])

    vt_scr[:, ATT_BLOCK:ATT_BLOCK + tm] = proj(OFF_AV, ATT_KV).astype(BF16)
    ka = zk[:, 0:ATT_KV]
    kband_scr[ATT_BLOCK:ATT_BLOCK + tm, :] = (
        ka * tok_ref[:, TOK_CKA:TOK_CKA + LANES]
        + pltpu.roll(ka, LANES // 2, 1) * tok_ref[:, TOK_SKA:TOK_SKA + LANES]).astype(BF16)
    item_aq(0)

    ncol = GQA_GROUP * ATT_BLOCK
    kj = lax.broadcasted_iota(jnp.int32, (2 * ATT_BLOCK, ncol), 0)
    qi = lax.broadcasted_iota(jnp.int32, (2 * ATT_BLOCK, ncol), 1) % ATT_BLOCK
    band = (kj > qi) & (kj <= qi + WINDOW)
    band_first = band & ((kj >= ATT_BLOCK) | (t > 0))
    sink_rows = [
        jnp.concatenate([jnp.full((1, ATT_BLOCK), sinks_ref[GQA_GROUP * j + g], F32)
                         for g in range(GQA_GROUP)], axis=1) for j in range(N_KV_HEADS)]

    def att_scores(c, j):
        r0 = c * ATT_BLOCK
        qs = jnp.concatenate(
            [qpad_scr[j, :, g * tm + r0:g * tm + r0 + ATT_BLOCK] for g in range(GQA_GROUP)],
            axis=1)
        kb = kband_scr[r0:r0 + 2 * ATT_BLOCK, :]
        return jnp.where(band_first if c == 0 else band, _dot(kb, qs), -jnp.inf)

    def att_finish(c, j, st):
        r0 = c * ATT_BLOCK
        m = jnp.maximum(jnp.max(st, axis=0, keepdims=True), sink_rows[j])
        p = jnp.exp(st - m)
        denom = jnp.sum(p, axis=0, keepdims=True) + jnp.exp(sink_rows[j] - m)
        vb = vt_scr[HEAD_DIM * j:HEAD_DIM * (j + 1), r0:r0 + 2 * ATT_BLOCK]
        ot = _dot(vb, p.astype(BF16)) / denom
        for g in range(GQA_GROUP):
            hh = GQA_GROUP * j + g
            oatt_scr[HEAD_DIM * hh:HEAD_DIM * (hh + 1), r0:r0 + ATT_BLOCK] = (
                ot[:, g * ATT_BLOCK:(g + 1) * ATT_BLOCK].astype(BF16))

    att_units = [(c, j) for j in range(N_KV_HEADS) for c in range(tm // ATT_BLOCK)]
    att_sched = spread(att_items, len(att_units))
    st_next = att_scores(*att_units[0])
    for u, unit in enumerate(att_units):
        st_cur = st_next
        if u + 1 < len(att_units):
            st_next = att_scores(*att_units[u + 1])
        for item in att_sched[u]:
            item()
        att_finish(*unit, st_cur)

    kband_scr[0:ATT_BLOCK, :] = kband_scr[tm:tm + ATT_BLOCK, :]
    vt_scr[:, 0:ATT_BLOCK] = vt_scr[:, tm:tm + ATT_BLOCK]

    def ret_scores(c, hd):
        r0 = c * RET_BLOCK
        q = rq_scr[RET_QK_DIM * hd:RET_QK_DIM * (hd + 1), r0:r0 + RET_BLOCK]
        kk = rk_scr[r0:r0 + RET_BLOCK, RET_QK_DIM * hd:RET_QK_DIM * (hd + 1)]
        return (_dot(kk, q) * dmask_ref[hd]).astype(BF16)

    def ret_finish(c, hd, s):
        r0 = c * RET_BLOCK
        q = rq_scr[RET_QK_DIM * hd:RET_QK_DIM * (hd + 1), r0:r0 + RET_BLOCK]
        kk = rk_scr[r0:r0 + RET_BLOCK, RET_QK_DIM * hd:RET_QK_DIM * (hd + 1)]
        vv = rv_scr[RET_V_DIM * hd:RET_V_DIM * (hd + 1), r0:r0 + RET_BLOCK]
        state = state_scr[hd]
        inner = _dot(vv.astype(BF16), s)
        cross = _dot(state.astype(BF16), q) * xi_ref[hd][0:1, :]
        out = inner + cross
        kv = _dot((vv * zeta_ref[hd][0:1, :]).astype(BF16), kk)
        state_scr[hd] = state * gch_ref[hd] + kv
        rn = out * lax.rsqrt(jnp.mean(out * out, axis=0, keepdims=True) + EPS)
        rb_scr[RET_V_DIM * hd:RET_V_DIM * (hd + 1), r0:r0 + RET_BLOCK] = (
            rn * silu_scr[RET_V_DIM * hd:RET_V_DIM * (hd + 1), r0:r0 + RET_BLOCK]).astype(BF16)

    ret_units = [(c, hd) for c in range(tm // RET_BLOCK) for hd in range(RET_HEADS)]
    ret_sched = spread(ret_items, len(ret_units))
    s_next = ret_scores(*ret_units[0])
    for u, unit in enumerate(ret_units):
        s_cur = s_next
        if u + 1 < len(ret_units):
            s_next = ret_scores(*ret_units[u + 1])
        for item in ret_sched[u]:
            item()
        ret_finish(*unit, s_cur)

    norm_to_scratch(xnext_ref[0])
    rb = rb_scr[...]
    for i in range(nblk):
        rows = slice(i * ROW_BLOCK, (i + 1) * ROW_BLOCK)
        yb = _dot(wbt_ref[rows, :], rb)
        mixedb_scr[rows, :] = (mixed_scr[rows, :] + sgb_scr[rows, :] * yb).astype(BF16)
    o_ref[0] = x_ref[0] + _dot_tn(mixedb_scr[...], wout_ref[...])


def _mlp_kernel(x_ref, g_ref, wup_ref, wdown_ref, gfin_ref, o_ref, *, final_norm):
    x = x_ref[...]
    h = _rmsnorm(x, g_ref[...]).astype(BF16)
    acc = jnp.zeros(x.shape, F32)
    for j in range(D_FF // FF_CHUNK):
        u = _dot(h, wup_ref[:, j * FF_CHUNK:(j + 1) * FF_CHUNK])
        a = jnp.square(jnp.maximum(u, 0.0)).astype(BF16)
        acc = acc + _dot(a, wdown_ref[j * FF_CHUNK:(j + 1) * FF_CHUNK, :])
    y = x + acc
    if final_norm:
        y = _rmsnorm(y, gfin_ref[...])
    o_ref[...] = y


def _resident(shape):
    return pl.BlockSpec(shape, lambda *_: (0,) * len(shape), pipeline_mode=pl.Buffered(1))


def _resident_layer(stacked, layer):
    tail = stacked.shape[1:]
    return pl.BlockSpec((None,) + tail, lambda *_: (layer,) + (0,) * len(tail),
                        pipeline_mode=pl.Buffered(1))


def _mixer_call(x, layer, g, wt, wrq, wk, wat, wbt, wout, sinks, gch, tables):
    bsz, seq, d = x.shape
    tm = min(SEQ_TILE, seq)
    feat_tab, tok_tab, dmask, zeta, xi = tables
    tok_spec = pl.BlockSpec((tm, tok_tab.shape[1]), lambda b, t: (t, 0))
    feat_spec = pl.BlockSpec((feat_tab.shape[0], tm), lambda b, t: (0, t))
    smem = pl.BlockSpec(memory_space=pltpu.SMEM)
    x_spec = pl.BlockSpec((1, tm, d), lambda b, t: (b, t, 0))
    nt = seq // tm

    def next_tile(b, t):
        flat = jnp.minimum(b * nt + t + 1, bsz * nt - 1)
        return (flat // nt, flat % nt, 0)

    return pl.pallas_call(
        _mixer_kernel,
        grid=(bsz, nt),
        in_specs=[x_spec, pl.BlockSpec((1, tm, d), next_tile)]
        + [_resident_layer(w, layer) for w in (g, wt, wrq, wk, wat, wbt, wout)]
        + [smem, smem, feat_spec, tok_spec,
           _resident(dmask.shape), _resident(zeta.shape), _resident(xi.shape)],
        out_specs=x_spec,
        out_shape=jax.ShapeDtypeStruct(x.shape, F32),
        scratch_shapes=[
            pltpu.VMEM((tm, d), BF16),
            pltpu.VMEM((d, tm), BF16),
            pltpu.VMEM((N_KV_HEADS, LANES, GQA_GROUP * tm), BF16),
            pltpu.VMEM((tm + ATT_BLOCK, LANES), BF16),
            pltpu.VMEM((ATT_KV, tm + ATT_BLOCK), BF16),
            pltpu.VMEM((ATT_Q, tm), BF16),
            pltpu.VMEM((RET_QK, tm), BF16),
            pltpu.VMEM((tm, RET_QK), BF16),
            pltpu.VMEM((RET_V, tm), F32),
            pltpu.VMEM((RET_V, tm), F32),
            pltpu.VMEM((RET_V, tm), BF16),
            pltpu.VMEM((RET_HEADS, RET_V_DIM, RET_QK_DIM), F32),
            pltpu.VMEM((d, tm), F32),
            pltpu.VMEM((d, tm), F32),
            pltpu.VMEM((d, tm), BF16),
        ],
        compiler_params=pltpu.CompilerParams(
            dimension_semantics=("arbitrary", "arbitrary"),
            vmem_limit_bytes=VMEM_LIMIT_BYTES),
        name="mixer",
    )(x, x, g, wt, wrq, wk, wat, wbt, wout, sinks, gch, feat_tab, tok_tab, dmask, zeta, xi)


def _mlp_call(x2d, layer, g, wup, wdown, gfin, final_norm):
    n, d = x2d.shape
    tm = min(MLP_TILE, n)
    x_spec = pl.BlockSpec((tm, d), lambda i: (i, 0))
    return pl.pallas_call(
        functools.partial(_mlp_kernel, final_norm=final_norm),
        grid=(n // tm,),
        in_specs=[x_spec] + [_resident_layer(w, layer) for w in (g, wup, wdown)]
        + [_resident((1, d))],
        out_specs=x_spec,
        out_shape=jax.ShapeDtypeStruct(x2d.shape, F32),
        compiler_params=pltpu.CompilerParams(
            dimension_semantics=("arbitrary",),
            vmem_limit_bytes=VMEM_LIMIT_BYTES),
        name="mlp_final" if final_norm else "mlp",
    )(x2d, g, wup, wdown, gfin)


def _even_odd_heads(w):
    lead = w.shape[:-1]
    w = w.reshape(lead + (RET_HEADS, RET_QK_DIM // 2, 2))
    return jnp.swapaxes(w, -1, -2).reshape(lead + (RET_QK,))


def _halves_first(w):
    lead = w.shape[:-1]
    w = w.reshape(lead + (N_KV_HEADS, 2, HALF))
    return jnp.swapaxes(w, -2, -3).reshape(lead + (ATT_KV,))


def _prepare_in_projection(w_in):
    wt = jnp.swapaxes(w_in, 1, 2).astype(BF16)
    wrq = jnp.swapaxes(_even_odd_heads(w_in[:, :, OFF_RQ:OFF_RK]), 1, 2).astype(BF16)
    wk = jnp.concatenate([_halves_first(w_in[:, :, OFF_AK:OFF_AV]),
                          _even_odd_heads(w_in[:, :, OFF_RK:OFF_RV])], axis=2).astype(BF16)
    return wt, wrq, wk


def _as_f32(a):
    return np.ascontiguousarray(a, dtype=np.float32)


@functools.lru_cache(maxsize=None)
def _position_tables(seq):
    pos = np.arange(seq, dtype=np.float64)
    inv = ROPE_THETA ** (-np.arange(HALF, dtype=np.float64) / HALF)
    ang = pos[:, None] * inv[None, :]
    cos, sin = np.cos(ang), np.sin(ang)
    theta = 1.0 / (RET_THETA ** np.linspace(0.0, 1.0, RET_QK_DIM // 2))
    angr = pos[:, None] * theta[None, :]
    cosr, sinr = np.cos(angr), np.sin(angr)
    tok_tab = np.concatenate([cos, cos, cos, cos, -sin, -sin, sin, sin,
                              cosr, cosr, -sinr, sinr], axis=1)
    feat_tab = np.concatenate([cos, sin, cosr, sinr], axis=1).T

    log_g = np.log(1.0 - 2.0 ** (-5.0 - np.arange(RET_HEADS, dtype=np.float64)))
    idx = np.arange(RET_BLOCK, dtype=np.float64)
    rel = idx[None, :] - idx[:, None]
    dmask = np.where(rel[None] >= 0, np.exp(log_g[:, None, None] * np.maximum(rel, 0.0)[None]), 0.0)
    zeta = np.exp(log_g[:, None] * (RET_BLOCK - 1.0 - idx)[None, :])
    xi = np.exp(log_g[:, None] * (idx + 1.0)[None, :])
    zeta = np.broadcast_to(zeta[:, None, :], (RET_HEADS, 8, RET_BLOCK))
    xi = np.broadcast_to(xi[:, None, :], (RET_HEADS, 8, RET_BLOCK))
    gch = np.exp(log_g * RET_BLOCK)
    return tuple(_as_f32(a) for a in (feat_tab, tok_tab, dmask, zeta, xi)), _as_f32(gch)


def kernel(x, g_mix, w_in, sinks, w_a, w_b, w_out, g_mlp, w_up, w_down, g_final):
    bsz, seq, d = x.shape
    depth = w_in.shape[0]
    tables, gch = _position_tables(seq)
    wt, wrq, wk = _prepare_in_projection(w_in)
    wat = jnp.swapaxes(w_a, 1, 2).astype(BF16)
    wbt = jnp.swapaxes(w_b, 1, 2).astype(BF16)
    wout = w_out.astype(BF16)
    wup = w_up.astype(BF16)
    wdown = w_down.astype(BF16)
    g_mix3 = g_mix[:, None, :]
    g_mlp3 = g_mlp[:, None, :]
    for l in range(depth):
        x = _mixer_call(x, l, g_mix3, wt, wrq, wk, wat, wbt, wout, sinks[l], gch, tables)
        x = _mlp_call(x.reshape(bsz * seq, d), l, g_mlp3, wup, wdown, g_final[None, :],
                      final_norm=(l == depth - 1)).reshape(bsz, seq, d)
    return x
```

```python
import functools

import numpy as np
import jax
import jax.numpy as jnp
from jax import lax
from jax.experimental import pallas as pl
from jax.experimental.pallas import tpu as pltpu

D_MODEL = 1024
HEAD_DIM = 64
N_Q_HEADS = 8
N_KV_HEADS = 2
GQA_GROUP = N_Q_HEADS // N_KV_HEADS
WINDOW = 128
ROPE_THETA = 10000.0
RET_HEADS = 4
RET_QK_DIM = 128
RET_V_DIM = 2 * RET_QK_DIM
RET_THETA = 10000.0
D_FF = 4 * D_MODEL
EPS = 1e-6
LOG2E = 1.4426950408889634

ATT_Q = N_Q_HEADS * HEAD_DIM
ATT_KV = N_KV_HEADS * HEAD_DIM
RET_QK = RET_HEADS * RET_QK_DIM
RET_V = RET_HEADS * RET_V_DIM

OFF_AQ = 0
OFF_AK = OFF_AQ + ATT_Q
OFF_AV = OFF_AK + ATT_KV
OFF_RQ = OFF_AV + ATT_KV
OFF_RK = OFF_RQ + RET_QK
OFF_RV = OFF_RK + RET_QK
OFF_RG = OFF_RV + RET_V
OFF_GA = OFF_RG + RET_V
OFF_GB = OFF_GA + D_MODEL
W_IN = OFF_GB + D_MODEL

TOK_CKA, TOK_SKA, TOK_CKR, TOK_SKR = 0, 128, 256, 384
FEAT_CQA = 0
FEAT_SQA = FEAT_CQA + HEAD_DIM // 2
FEAT_CQR = FEAT_SQA + HEAD_DIM // 2
FEAT_SQR = FEAT_CQR + RET_QK_DIM // 2
FEAT_END = FEAT_SQR + RET_QK_DIM // 2

LANES = 128
HALF = HEAD_DIM // 2
ATT_BLOCK = WINDOW
RET_BLOCK = 256
SEQ_TILE = 512
ROW_BLOCK = 256
MLP_TILE = 1024
FF_CHUNK = 1024
VMEM_LIMIT_BYTES = 56 * 1024 * 1024

BF16 = jnp.bfloat16
F32 = jnp.float32


def _dot(a, b):
    return jnp.dot(a, b, preferred_element_type=F32)


def _dot_nt(a, b):
    return lax.dot_general(a, b, (((1,), (1,)), ((), ())), preferred_element_type=F32)


def _dot_tn(a, b):
    return lax.dot_general(a, b, (((0,), (0,)), ((), ())), preferred_element_type=F32)


def _sigmoid(x):
    return 1.0 / (1.0 + jnp.exp2(x * (-LOG2E)))


def _rmsnorm(x, g):
    return x * lax.rsqrt(jnp.mean(x * x, axis=-1, keepdims=True) + EPS) * g


def _mixer_kernel(x_ref, xnext_ref, g_ref, wt_ref, wrq_ref, wk_ref, wat_ref, wbt_ref, wout_ref,
                  sinks_ref, gch_ref, feat_ref, tok_ref, band_ref, dmask_ref, zeta_ref, xi_ref,
                  o_ref,
                  h_scr, ht_scr, qpad_scr, kband_scr, vt_scr, oatt_scr,
                  rq_scr, rk_scr, rv_scr, silu_scr, rb_scr, state_scr, mixed_scr, sgb_scr, mixedb_scr):
    tm = x_ref.shape[1]
    t = pl.program_id(1)

    @pl.when(t == 0)
    def _():
        kband_scr[0:ATT_BLOCK, :] = jnp.zeros((ATT_BLOCK, LANES), BF16)
        vt_scr[:, 0:ATT_BLOCK] = jnp.zeros((ATT_KV, ATT_BLOCK), BF16)
        qpad_scr[...] = jnp.zeros(qpad_scr.shape, BF16)
        state_scr[...] = jnp.zeros(state_scr.shape, F32)

    def norm_to_scratch(xv):
        hb = _rmsnorm(xv, g_ref[...]).astype(BF16)
        h_scr[...] = hb
        ht_scr[...] = hb.T

    @pl.when((pl.program_id(0) == 0) & (t == 0))
    def _():
        norm_to_scratch(x_ref[0])

    zk = _dot(h_scr[...], wk_ref[...])

    def proj(r0, nrows):
        return _dot(wt_ref[r0:r0 + nrows, :], ht_scr[...])

    kscale = RET_QK_DIM ** -0.5
    hq = RET_QK_DIM // 2

    def item_rk(hd):
        kk = zk[:, ATT_KV + RET_QK_DIM * hd:ATT_KV + RET_QK_DIM * (hd + 1)]
        rk_scr[:, RET_QK_DIM * hd:RET_QK_DIM * (hd + 1)] = (
            (kk * tok_ref[:, TOK_CKR:TOK_CKR + LANES]
             + pltpu.roll(kk, LANES // 2, 1) * tok_ref[:, TOK_SKR:TOK_SKR + LANES]) * kscale).astype(BF16)

    def item_rq(i):
        blk = _dot(wrq_ref[i * ROW_BLOCK:(i + 1) * ROW_BLOCK, :], ht_scr[...])
        cqr = feat_ref[FEAT_CQR:FEAT_SQR, :]
        sqr = feat_ref[FEAT_SQR:FEAT_END, :]
        for hl in range(ROW_BLOCK // RET_QK_DIM):
            base = i * ROW_BLOCK + hl * RET_QK_DIM
            ev = blk[hl * RET_QK_DIM:hl * RET_QK_DIM + hq]
            od = blk[hl * RET_QK_DIM + hq:(hl + 1) * RET_QK_DIM]
            rq_scr[base:base + hq, :] = (ev * cqr - od * sqr).astype(BF16)
            rq_scr[base + hq:base + RET_QK_DIM, :] = (od * cqr + ev * sqr).astype(BF16)

    def item_rv(i):
        rv_scr[i * ROW_BLOCK:(i + 1) * ROW_BLOCK, :] = proj(OFF_RV + i * ROW_BLOCK, ROW_BLOCK)

    def item_rg(i):
        rg = proj(OFF_RG + i * ROW_BLOCK, ROW_BLOCK)
        silu_scr[i * ROW_BLOCK:(i + 1) * ROW_BLOCK, :] = rg * _sigmoid(rg)

    def item_ga(i):
        ga = proj(OFF_GA + i * ROW_BLOCK, ROW_BLOCK)
        ya = _dot(wat_ref[i * ROW_BLOCK:(i + 1) * ROW_BLOCK, :], oatt_scr[...])
        mixed_scr[i * ROW_BLOCK:(i + 1) * ROW_BLOCK, :] = _sigmoid(ga) * ya

    def item_gb(i):
        sgb_scr[i * ROW_BLOCK:(i + 1) * ROW_BLOCK, :] = _sigmoid(proj(OFF_GB + i * ROW_BLOCK, ROW_BLOCK))

    def spread(items, n_units):
        cuts = [round(u * len(items) / n_units) for u in range(n_units + 1)]
        return [items[cuts[u]:cuts[u + 1]] for u in range(n_units)]

    scale = HEAD_DIM ** -0.5 * LOG2E

    def item_aq(j):
        qt = proj(OFF_AQ + j * GQA_GROUP * HEAD_DIM, GQA_GROUP * HEAD_DIM)
        cq = feat_ref[FEAT_CQA:FEAT_SQA, :]
        sq = feat_ref[FEAT_SQA:FEAT_CQR, :]
        for g in range(GQA_GROUP):
            x1 = qt[HEAD_DIM * g:HEAD_DIM * g + HALF]
            x2 = qt[HEAD_DIM * g + HALF:HEAD_DIM * (g + 1)]
            qpad_scr[j, HALF * j:HALF * (j + 1), g * tm:(g + 1) * tm] = (
                (x1 * cq - x2 * sq) * scale).astype(BF16)
            qpad_scr[j, 2 * HALF + HALF * j:2 * HALF + HALF * (j + 1), g * tm:(g + 1) * tm] = (
                (x2 * cq + x1 * sq) * scale).astype(BF16)

    nblk = D_MODEL // ROW_BLOCK
    att_items = ([functools.partial(item_aq, j) for j in range(1, N_KV_HEADS)]
                 + [functools.partial(item_rk, hd) for hd in range(RET_HEADS)]
                 + [functools.partial(item_rq, i) for i in range(RET_QK // ROW_BLOCK)]
                 + [functools.partial(item_rv, i) for i in range(RET_V // ROW_BLOCK)]
                 + [functools.partial(item_rg, i) for i in range(RET_V // ROW_BLOCK)])
    ret_items = ([functools.partial(item_ga, i) for i in range(nblk)]
                 + [functools.partial(item_gb, i) for i in range(nblk)])

    vt_scr[:, ATT_BLOCK:ATT_BLOCK + tm] = proj(OFF_AV, ATT_KV).astype(BF16)
    ka = zk[:, 0:ATT_KV]
    kband_scr[ATT_BLOCK:ATT_BLOCK + tm, :] = (
        ka * tok_ref[:, TOK_CKA:TOK_CKA + LANES]
        + pltpu.roll(ka, LANES // 2, 1) * tok_ref[:, TOK_SKA:TOK_SKA + LANES]).astype(BF16)
    item_aq(0)

    band = band_ref[0]
    band_first = jnp.where(t > 0, band, band_ref[1])
    sink_rows = [
        jnp.concatenate([jnp.full((1, ATT_BLOCK), sinks_ref[GQA_GROUP * j + g] * LOG2E, F32)
                         for g in range(GQA_GROUP)], axis=1) for j in range(N_KV_HEADS)]

    def att_scores(c, j):
        r0 = c * ATT_BLOCK
        qs = jnp.concatenate(
            [qpad_scr[j, :, g * tm + r0:g * tm + r0 + ATT_BLOCK] for g in range(GQA_GROUP)],
            axis=1)
        kb = kband_scr[r0:r0 + 2 * ATT_BLOCK, :]
        return _dot(kb, qs) + (band_first if c == 0 else band)

    def att_finish(c, j, st):
        r0 = c * ATT_BLOCK
        m = jnp.maximum(jnp.max(st, axis=0, keepdims=True), sink_rows[j])
        p = jnp.exp2(st - m)
        denom = jnp.sum(p, axis=0, keepdims=True) + jnp.exp2(sink_rows[j] - m)
        vb = vt_scr[HEAD_DIM * j:HEAD_DIM * (j + 1), r0:r0 + 2 * ATT_BLOCK]
        ot = _dot(vb, p.astype(BF16)) / denom
        for g in range(GQA_GROUP):
            hh = GQA_GROUP * j + g
            oatt_scr[HEAD_DIM * hh:HEAD_DIM * (hh + 1), r0:r0 + ATT_BLOCK] = (
                ot[:, g * ATT_BLOCK:(g + 1) * ATT_BLOCK].astype(BF16))

    att_units = [(c, j) for j in range(N_KV_HEADS) for c in range(tm // ATT_BLOCK)]
    att_sched = spread(att_items, len(att_units))
    st_next = att_scores(*att_units[0])
    for u, unit in enumerate(att_units):
        st_cur = st_next
        if u + 1 < len(att_units):
            st_next = att_scores(*att_units[u + 1])
        for item in att_sched[u]:
            item()
        att_finish(*unit, st_cur)

    kband_scr[0:ATT_BLOCK, :] = kband_scr[tm:tm + ATT_BLOCK, :]
    vt_scr[:, 0:ATT_BLOCK] = vt_scr[:, tm:tm + ATT_BLOCK]

    def ret_scores(c, hd):
        r0 = c * RET_BLOCK
        q = rq_scr[RET_QK_DIM * hd:RET_QK_DIM * (hd + 1), r0:r0 + RET_BLOCK]
        kk = rk_scr[r0:r0 + RET_BLOCK, RET_QK_DIM * hd:RET_QK_DIM * (hd + 1)]
        return (_dot(kk, q) * dmask_ref[hd]).astype(BF16)

    def ret_finish(c, hd, s):
        r0 = c * RET_BLOCK
        q = rq_scr[RET_QK_DIM * hd:RET_QK_DIM * (hd + 1), r0:r0 + RET_BLOCK]
        kk = rk_scr[r0:r0 + RET_BLOCK, RET_QK_DIM * hd:RET_QK_DIM * (hd + 1)]
        vv = rv_scr[RET_V_DIM * hd:RET_V_DIM * (hd + 1), r0:r0 + RET_BLOCK]
        state = state_scr[hd]
        inner = _dot(vv.astype(BF16), s)
        cross = _dot(state.astype(BF16), q) * xi_ref[hd][0:1, :]
        out = inner + cross
        kv = _dot((vv * zeta_ref[hd][0:1, :]).astype(BF16), kk)
        state_scr[hd] = state * gch_ref[hd] + kv
        rn = out * lax.rsqrt(jnp.mean(out * out, axis=0, keepdims=True) + EPS)
        rb_scr[RET_V_DIM * hd:RET_V_DIM * (hd + 1), r0:r0 + RET_BLOCK] = (
            rn * silu_scr[RET_V_DIM * hd:RET_V_DIM * (hd + 1), r0:r0 + RET_BLOCK]).astype(BF16)

    ret_units = [(c, hd) for c in range(tm // RET_BLOCK) for hd in range(RET_HEADS)]
    ret_sched = spread(ret_items, len(ret_units))
    s_next = ret_scores(*ret_units[0])
    for u, unit in enumerate(ret_units):
        s_cur = s_next
        if u + 1 < len(ret_units):
            s_next = ret_scores(*ret_units[u + 1])
        for item in ret_sched[u]:
            item()
        ret_finish(*unit, s_cur)

    norm_to_scratch(xnext_ref[0])
    rb = rb_scr[...]
    for i in range(nblk):
        rows = slice(i * ROW_BLOCK, (i + 1) * ROW_BLOCK)
        yb = _dot(wbt_ref[rows, :], rb)
        mixedb_scr[rows, :] = (mixed_scr[rows, :] + sgb_scr[rows, :] * yb).astype(BF16)
    o_ref[0] = x_ref[0] + _dot_tn(mixedb_scr[...], wout_ref[...])


def _mlp_kernel(x_ref, g_ref, wup_ref, wdown_ref, gfin_ref, o_ref, *, final_norm):
    x = x_ref[...]
    h = _rmsnorm(x, g_ref[...]).astype(BF16)
    acc = jnp.zeros(x.shape, F32)
    for j in range(D_FF // FF_CHUNK):
        u = _dot(h, wup_ref[:, j * FF_CHUNK:(j + 1) * FF_CHUNK])
        a = jnp.square(jnp.maximum(u, 0.0)).astype(BF16)
        acc = acc + _dot(a, wdown_ref[j * FF_CHUNK:(j + 1) * FF_CHUNK, :])
    y = x + acc
    if final_norm:
        y = _rmsnorm(y, gfin_ref[...])
    o_ref[...] = y


def _resident(shape):
    return pl.BlockSpec(shape, lambda *_: (0,) * len(shape), pipeline_mode=pl.Buffered(1))


def _resident_layer(stacked, layer):
    tail = stacked.shape[1:]
    return pl.BlockSpec((None,) + tail, lambda *_: (layer,) + (0,) * len(tail),
                        pipeline_mode=pl.Buffered(1))


def _mixer_call(x, layer, g, wt, wrq, wk, wat, wbt, wout, sinks, gch, tables):
    bsz, seq, d = x.shape
    tm = min(SEQ_TILE, seq)
    feat_tab, tok_tab, band, dmask, zeta, xi = tables
    tok_spec = pl.BlockSpec((tm, tok_tab.shape[1]), lambda b, t: (t, 0))
    feat_spec = pl.BlockSpec((feat_tab.shape[0], tm), lambda b, t: (0, t))
    smem = pl.BlockSpec(memory_space=pltpu.SMEM)
    x_spec = pl.BlockSpec((1, tm, d), lambda b, t: (b, t, 0))
    nt = seq // tm

    def next_tile(b, t):
        flat = jnp.minimum(b * nt + t + 1, bsz * nt - 1)
        return (flat // nt, flat % nt, 0)

    return pl.pallas_call(
        _mixer_kernel,
        grid=(bsz, nt),
        in_specs=[x_spec, pl.BlockSpec((1, tm, d), next_tile)]
        + [_resident_layer(w, layer) for w in (g, wt, wrq, wk, wat, wbt, wout)]
        + [smem, smem, feat_spec, tok_spec,
           _resident(band.shape), _resident(dmask.shape), _resident(zeta.shape), _resident(xi.shape)],
        out_specs=x_spec,
        out_shape=jax.ShapeDtypeStruct(x.shape, F32),
        scratch_shapes=[
            pltpu.VMEM((tm, d), BF16),
            pltpu.VMEM((d, tm), BF16),
            pltpu.VMEM((N_KV_HEADS, LANES, GQA_GROUP * tm), BF16),
            pltpu.VMEM((tm + ATT_BLOCK, LANES), BF16),
            pltpu.VMEM((ATT_KV, tm + ATT_BLOCK), BF16),
            pltpu.VMEM((ATT_Q, tm), BF16),
            pltpu.VMEM((RET_QK, tm), BF16),
            pltpu.VMEM((tm, RET_QK), BF16),
            pltpu.VMEM((RET_V, tm), F32),
            pltpu.VMEM((RET_V, tm), F32),
            pltpu.VMEM((RET_V, tm), BF16),
            pltpu.VMEM((RET_HEADS, RET_V_DIM, RET_QK_DIM), F32),
            pltpu.VMEM((d, tm), F32),
            pltpu.VMEM((d, tm), F32),
            pltpu.VMEM((d, tm), BF16),
        ],
        compiler_params=pltpu.CompilerParams(
            dimension_semantics=("arbitrary", "arbitrary"),
            vmem_limit_bytes=VMEM_LIMIT_BYTES),
        name="mixer",
    )(x, x, g, wt, wrq, wk, wat, wbt, wout, sinks, gch, feat_tab, tok_tab, band, dmask, zeta, xi)


def _mlp_call(x2d, layer, g, wup, wdown, gfin, final_norm):
    n, d = x2d.shape
    tm = min(MLP_TILE, n)
    x_spec = pl.BlockSpec((tm, d), lambda i: (i, 0))
    return pl.pallas_call(
        functools.partial(_mlp_kernel, final_norm=final_norm),
        grid=(n // tm,),
        in_specs=[x_spec] + [_resident_layer(w, layer) for w in (g, wup, wdown)]
        + [_resident((1, d))],
        out_specs=x_spec,
        out_shape=jax.ShapeDtypeStruct(x2d.shape, F32),
        compiler_params=pltpu.CompilerParams(
            dimension_semantics=("arbitrary",),
            vmem_limit_bytes=VMEM_LIMIT_BYTES),
        name="mlp_final" if final_norm else "mlp",
    )(x2d, g, wup, wdown, gfin)


def _even_odd_heads(w):
    lead = w.shape[:-1]
    w = w.reshape(lead + (RET_HEADS, RET_QK_DIM // 2, 2))
    return jnp.swapaxes(w, -1, -2).reshape(lead + (RET_QK,))


def _halves_first(w):
    lead = w.shape[:-1]
    w = w.reshape(lead + (N_KV_HEADS, 2, HALF))
    return jnp.swapaxes(w, -2, -3).reshape(lead + (ATT_KV,))


def _prepare_in_projection(w_in):
    wt = jnp.swapaxes(w_in, 1, 2).astype(BF16)
    wrq = jnp.swapaxes(_even_odd_heads(w_in[:, :, OFF_RQ:OFF_RK]), 1, 2).astype(BF16)
    wk = jnp.concatenate([_halves_first(w_in[:, :, OFF_AK:OFF_AV]),
                          _even_odd_heads(w_in[:, :, OFF_RK:OFF_RV])], axis=2).astype(BF16)
    return wt, wrq, wk


def _as_f32(a):
    return np.ascontiguousarray(a, dtype=np.float32)


@functools.lru_cache(maxsize=None)
def _position_tables(seq):
    pos = np.arange(seq, dtype=np.float64)
    inv = ROPE_THETA ** (-np.arange(HALF, dtype=np.float64) / HALF)
    ang = pos[:, None] * inv[None, :]
    cos, sin = np.cos(ang), np.sin(ang)
    theta = 1.0 / (RET_THETA ** np.linspace(0.0, 1.0, RET_QK_DIM // 2))
    angr = pos[:, None] * theta[None, :]
    cosr, sinr = np.cos(angr), np.sin(angr)
    tok_tab = np.concatenate([cos, cos, cos, cos, -sin, -sin, sin, sin,
                              cosr, cosr, -sinr, sinr], axis=1)
    feat_tab = np.concatenate([cos, sin, cosr, sinr], axis=1).T

    kj = np.arange(2 * ATT_BLOCK)[:, None]
    qi = np.arange(GQA_GROUP * ATT_BLOCK)[None, :] % ATT_BLOCK
    inside = (kj > qi) & (kj <= qi + WINDOW)
    band = np.where(np.stack([inside, inside & (kj >= ATT_BLOCK)]), 0.0, -np.inf)

    log_g = np.log(1.0 - 2.0 ** (-5.0 - np.arange(RET_HEADS, dtype=np.float64)))
    idx = np.arange(RET_BLOCK, dtype=np.float64)
    rel = idx[None, :] - idx[:, None]
    dmask = np.where(rel[None] >= 0, np.exp(log_g[:, None, None] * np.maximum(rel, 0.0)[None]), 0.0)
    zeta = np.exp(log_g[:, None] * (RET_BLOCK - 1.0 - idx)[None, :])
    xi = np.exp(log_g[:, None] * (idx + 1.0)[None, :])
    zeta = np.broadcast_to(zeta[:, None, :], (RET_HEADS, 8, RET_BLOCK))
    xi = np.broadcast_to(xi[:, None, :], (RET_HEADS, 8, RET_BLOCK))
    gch = np.exp(log_g * RET_BLOCK)
    return tuple(_as_f32(a) for a in (feat_tab, tok_tab, band, dmask, zeta, xi)), _as_f32(gch)


def kernel(x, g_mix, w_in, sinks, w_a, w_b, w_out, g_mlp, w_up, w_down, g_final):
    bsz, seq, d = x.shape
    depth = w_in.shape[0]
    tables, gch = _position_tables(seq)
    wt, wrq, wk = _prepare_in_projection(w_in)
    wat = jnp.swapaxes(w_a, 1, 2).astype(BF16)
    wbt = jnp.swapaxes(w_b, 1, 2).astype(BF16)
    wout = w_out.astype(BF16)
    wup = w_up.astype(BF16)
    wdown = w_down.astype(BF16)
    g_mix3 = g_mix[:, None, :]
    g_mlp3 = g_mlp[:, None, :]
    for l in range(depth):
        x = _mixer_call(x, l, g_mix3, wt, wrq, wk, wat, wbt, wout, sinks[l], gch, tables)
        x = _mlp_call(x.reshape(bsz * seq, d), l, g_mlp3, wup, wdown, g_final[None, :],
                      final_norm=(l == depth - 1)).reshape(bsz, seq, d)
    return x
```

```python
import functools

import numpy as np
import jax
import jax.numpy as jnp
from jax import lax
from jax.experimental import pallas as pl
from jax.experimental.pallas import tpu as pltpu

D_MODEL = 1024
HEAD_DIM = 64
N_Q_HEADS = 8
N_KV_HEADS = 2
GQA_GROUP = N_Q_HEADS // N_KV_HEADS
WINDOW = 128
ROPE_THETA = 10000.0
RET_HEADS = 4
RET_QK_DIM = 128
RET_V_DIM = 2 * RET_QK_DIM
RET_THETA = 10000.0
D_FF = 4 * D_MODEL
EPS = 1e-6
LOG2E = 1.4426950408889634

ATT_Q = N_Q_HEADS * HEAD_DIM
ATT_KV = N_KV_HEADS * HEAD_DIM
RET_QK = RET_HEADS * RET_QK_DIM
RET_V = RET_HEADS * RET_V_DIM

OFF_AQ = 0
OFF_AK = OFF_AQ + ATT_Q
OFF_AV = OFF_AK + ATT_KV
OFF_RQ = OFF_AV + ATT_KV
OFF_RK = OFF_RQ + RET_QK
OFF_RV = OFF_RK + RET_QK
OFF_RG = OFF_RV + RET_V
OFF_GA = OFF_RG + RET_V
OFF_GB = OFF_GA + D_MODEL
W_IN = OFF_GB + D_MODEL

TOK_CKA, TOK_SKA, TOK_CKR, TOK_SKR = 0, 128, 256, 384
FEAT_CQA = 0
FEAT_SQA = FEAT_CQA + HEAD_DIM // 2
FEAT_CQR = FEAT_SQA + HEAD_DIM // 2
FEAT_SQR = FEAT_CQR + RET_QK_DIM // 2
FEAT_END = FEAT_SQR + RET_QK_DIM // 2

LANES = 128
HALF = HEAD_DIM // 2
ATT_BLOCK = WINDOW
RET_BLOCK = 256
SEQ_TILE = 512
ROW_BLOCK = 256
CAST_BLOCKS = 16
MLP_TILE = 1024
FF_CHUNK = 1024
VMEM_LIMIT_BYTES = 56 * 1024 * 1024

BF16 = jnp.bfloat16
F32 = jnp.float32


def _dot(a, b):
    return jnp.dot(a, b, preferred_element_type=F32)


def _dot_nt(a, b):
    return lax.dot_general(a, b, (((1,), (1,)), ((), ())), preferred_element_type=F32)


def _dot_tn(a, b):
    return lax.dot_general(a, b, (((0,), (0,)), ((), ())), preferred_element_type=F32)


def _sigmoid(x):
    return 1.0 / (1.0 + jnp.exp2(x * (-LOG2E)))


def _rmsnorm(x, g):
    return x * lax.rsqrt(jnp.mean(x * x, axis=-1, keepdims=True) + EPS) * g


def _mixer_kernel(x_ref, xnext_ref, g_ref, wt_ref, wrq_ref, wk_ref, wat_ref, wbt_ref, wout_ref,
                  sinks_ref, gch_ref, feat_ref, tok_ref, band_ref, dmask_ref, zeta_ref, xi_ref,
                  wup_f32_ref, wdown_f32_ref,
                  o_ref, wup_bf16_ref, wdown_bf16_ref,
                  h_scr, ht_scr, qpad_scr, kband_scr, vt_scr, oatt_scr,
                  rq_scr, rk_scr, rv_scr, silu_scr, rb_scr, state_scr, mixed_scr, sgb_scr, mixedb_scr,
                  *, cast_every):
    tm = x_ref.shape[1]
    t = pl.program_id(1)

    @pl.when(t == 0)
    def _():
        kband_scr[0:ATT_BLOCK, :] = jnp.zeros((ATT_BLOCK, LANES), BF16)
        vt_scr[:, 0:ATT_BLOCK] = jnp.zeros((ATT_KV, ATT_BLOCK), BF16)
        qpad_scr[...] = jnp.zeros(qpad_scr.shape, BF16)
        state_scr[...] = jnp.zeros(state_scr.shape, F32)

    @pl.when((pl.program_id(0) * pl.num_programs(1) + t) % cast_every == 0)
    def _():
        wup_bf16_ref[...] = wup_f32_ref[...].astype(BF16)
        wdown_bf16_ref[...] = wdown_f32_ref[...].astype(BF16)

    def norm_to_scratch(xv):
        hb = _rmsnorm(xv, g_ref[...]).astype(BF16)
        h_scr[...] = hb
        ht_scr[...] = hb.T

    @pl.when((pl.program_id(0) == 0) & (t == 0))
    def _():
        norm_to_scratch(x_ref[0])

    zk = _dot(h_scr[...], wk_ref[...])

    def proj(r0, nrows):
        return _dot(wt_ref[r0:r0 + nrows, :], ht_scr[...])

    kscale = RET_QK_DIM ** -0.5
    hq = RET_QK_DIM // 2

    def item_rk(hd):
        kk = zk[:, ATT_KV + RET_QK_DIM * hd:ATT_KV + RET_QK_DIM * (hd + 1)]
        rk_scr[:, RET_QK_DIM * hd:RET_QK_DIM * (hd + 1)] = (
            (kk * tok_ref[:, TOK_CKR:TOK_CKR + LANES]
             + pltpu.roll(kk, LANES // 2, 1) * tok_ref[:, TOK_SKR:TOK_SKR + LANES]) * kscale).astype(BF16)

    def item_rq(i):
        blk = _dot(wrq_ref[i * ROW_BLOCK:(i + 1) * ROW_BLOCK, :], ht_scr[...])
        cqr = feat_ref[FEAT_CQR:FEAT_SQR, :]
        sqr = feat_ref[FEAT_SQR:FEAT_END, :]
        for hl in range(ROW_BLOCK // RET_QK_DIM):
            base = i * ROW_BLOCK + hl * RET_QK_DIM
            ev = blk[hl * RET_QK_DIM:hl * RET_QK_DIM + hq]
            od = blk[hl * RET_QK_DIM + hq:(hl + 1) * RET_QK_DIM]
            rq_scr[base:base + hq, :] = (ev * cqr - od * sqr).astype(BF16)
            rq_scr[base + hq:base + RET_QK_DIM, :] = (od * cqr + ev * sqr).astype(BF16)

    def item_rv(i):
        rv_scr[i * ROW_BLOCK:(i + 1) * ROW_BLOCK, :] = proj(OFF_RV + i * ROW_BLOCK, ROW_BLOCK)

    def item_rg(i):
        rg = proj(OFF_RG + i * ROW_BLOCK, ROW_BLOCK)
        silu_scr[i * ROW_BLOCK:(i + 1) * ROW_BLOCK, :] = rg * _sigmoid(rg)

    def item_ga(i):
        ga = proj(OFF_GA + i * ROW_BLOCK, ROW_BLOCK)
        ya = _dot(wat_ref[i * ROW_BLOCK:(i + 1) * ROW_BLOCK, :], oatt_scr[...])
        mixed_scr[i * ROW_BLOCK:(i + 1) * ROW_BLOCK, :] = _sigmoid(ga) * ya

    def item_gb(i):
        sgb_scr[i * ROW_BLOCK:(i + 1) * ROW_BLOCK, :] = _sigmoid(proj(OFF_GB + i * ROW_BLOCK, ROW_BLOCK))

    def spread(items, n_units):
        cuts = [round(u * len(items) / n_units) for u in range(n_units + 1)]
        return [items[cuts[u]:cuts[u + 1]] for u in range(n_units)]

    scale = HEAD_DIM ** -0.5 * LOG2E

    def item_aq(j):
        qt = proj(OFF_AQ + j * GQA_GROUP * HEAD_DIM, GQA_GROUP * HEAD_DIM)
        cq = feat_ref[FEAT_CQA:FEAT_SQA, :]
        sq = feat_ref[FEAT_SQA:FEAT_CQR, :]
        for g in range(GQA_GROUP):
            x1 = qt[HEAD_DIM * g:HEAD_DIM * g + HALF]
            x2 = qt[HEAD_DIM * g + HALF:HEAD_DIM * (g + 1)]
            qpad_scr[j, HALF * j:HALF * (j + 1), g * tm:(g + 1) * tm] = (
                (x1 * cq - x2 * sq) * scale).astype(BF16)
            qpad_scr[j, 2 * HALF + HALF * j:2 * HALF + HALF * (j + 1), g * tm:(g + 1) * tm] = (
                (x2 * cq + x1 * sq) * scale).astype(BF16)

    nblk = D_MODEL // ROW_BLOCK
    att_items = ([functools.partial(item_aq, j) for j in range(1, N_KV_HEADS)]
                 + [functools.partial(item_rk, hd) for hd in range(RET_HEADS)]
                 + [functools.partial(item_rq, i) for i in range(RET_QK // ROW_BLOCK)]
                 + [functools.partial(item_rv, i) for i in range(RET_V // ROW_BLOCK)]
                 + [functools.partial(item_rg, i) for i in range(RET_V // ROW_BLOCK)])
    ret_items = ([functools.partial(item_ga, i) for i in range(nblk)]
                 + [functools.partial(item_gb, i) for i in range(nblk)])

    vt_scr[:, ATT_BLOCK:ATT_BLOCK + tm] = proj(OFF_AV, ATT_KV).astype(BF16)
    ka = zk[:, 0:ATT_KV]
    kband_scr[ATT_BLOCK:ATT_BLOCK + tm, :] = (
        ka * tok_ref[:, TOK_CKA:TOK_CKA + LANES]
        + pltpu.roll(ka, LANES // 2, 1) * tok_ref[:, TOK_SKA:TOK_SKA + LANES]).astype(BF16)
    item_aq(0)

    band = band_ref[0]
    band_first = jnp.where(t > 0, band, band_ref[1])
    sink_rows = [
        jnp.concatenate([jnp.full((1, ATT_BLOCK), sinks_ref[GQA_GROUP * j + g] * LOG2E, F32)
                         for g in range(GQA_GROUP)], axis=1) for j in range(N_KV_HEADS)]

    def att_scores(c, j):
        r0 = c * ATT_BLOCK
        qs = jnp.concatenate(
            [qpad_scr[j, :, g * tm + r0:g * tm + r0 + ATT_BLOCK] for g in range(GQA_GROUP)],
            axis=1)
        kb = kband_scr[r0:r0 + 2 * ATT_BLOCK, :]
        return _dot(kb, qs) + (band_first if c == 0 else band)

    def att_finish(c, j, st):
        r0 = c * ATT_BLOCK
        m = jnp.maximum(jnp.max(st, axis=0, keepdims=True), sink_rows[j])
        p = jnp.exp2(st - m)
        denom = jnp.sum(p, axis=0, keepdims=True) + jnp.exp2(sink_rows[j] - m)
        vb = vt_scr[HEAD_DIM * j:HEAD_DIM * (j + 1), r0:r0 + 2 * ATT_BLOCK]
        ot = _dot(vb, p.astype(BF16)) / denom
        for g in range(GQA_GROUP):
            hh = GQA_GROUP * j + g
            oatt_scr[HEAD_DIM * hh:HEAD_DIM * (hh + 1), r0:r0 + ATT_BLOCK] = (
                ot[:, g * ATT_BLOCK:(g + 1) * ATT_BLOCK].astype(BF16))

    att_units = [(c, j) for j in range(N_KV_HEADS) for c in range(tm // ATT_BLOCK)]
    att_sched = spread(att_items, len(att_units))
    st_next = att_scores(*att_units[0])
    for u, unit in enumerate(att_units):
        st_cur = st_next
        if u + 1 < len(att_units):
            st_next = att_scores(*att_units[u + 1])
        for item in att_sched[u]:
            item()
        att_finish(*unit, st_cur)

    kband_scr[0:ATT_BLOCK, :] = kband_scr[tm:tm + ATT_BLOCK, :]
    vt_scr[:, 0:ATT_BLOCK] = vt_scr[:, tm:tm + ATT_BLOCK]

    def ret_scores(c, hd):
        r0 = c * RET_BLOCK
        q = rq_scr[RET_QK_DIM * hd:RET_QK_DIM * (hd + 1), r0:r0 + RET_BLOCK]
        kk = rk_scr[r0:r0 + RET_BLOCK, RET_QK_DIM * hd:RET_QK_DIM * (hd + 1)]
        return (_dot(kk, q) * dmask_ref[hd]).astype(BF16)

    def ret_finish(c, hd, s):
        r0 = c * RET_BLOCK
        q = rq_scr[RET_QK_DIM * hd:RET_QK_DIM * (hd + 1), r0:r0 + RET_BLOCK]
        kk = rk_scr[r0:r0 + RET_BLOCK, RET_QK_DIM * hd:RET_QK_DIM * (hd + 1)]
        vv = rv_scr[RET_V_DIM * hd:RET_V_DIM * (hd + 1), r0:r0 + RET_BLOCK]
        state = state_scr[hd]
        inner = _dot(vv.astype(BF16), s)
        cross = _dot(state.astype(BF16), q) * xi_ref[hd][0:1, :]
        out = inner + cross
        kv = _dot((vv * zeta_ref[hd][0:1, :]).astype(BF16), kk)
        state_scr[hd] = state * gch_ref[hd] + kv
        rn = out * lax.rsqrt(jnp.mean(out * out, axis=0, keepdims=True) + EPS)
        rb_scr[RET_V_DIM * hd:RET_V_DIM * (hd + 1), r0:r0 + RET_BLOCK] = (
            rn * silu_scr[RET_V_DIM * hd:RET_V_DIM * (hd + 1), r0:r0 + RET_BLOCK]).astype(BF16)

    ret_units = [(c, hd) for c in range(tm // RET_BLOCK) for hd in range(RET_HEADS)]
    ret_sched = spread(ret_items, len(ret_units))
    s_next = ret_scores(*ret_units[0])
    for u, unit in enumerate(ret_units):
        s_cur = s_next
        if u + 1 < len(ret_units):
            s_next = ret_scores(*ret_units[u + 1])
        for item in ret_sched[u]:
            item()
        ret_finish(*unit, s_cur)

    norm_to_scratch(xnext_ref[0])
    rb = rb_scr[...]
    for i in range(nblk):
        rows = slice(i * ROW_BLOCK, (i + 1) * ROW_BLOCK)
        yb = _dot(wbt_ref[rows, :], rb)
        mixedb_scr[rows, :] = (mixed_scr[rows, :] + sgb_scr[rows, :] * yb).astype(BF16)
    o_ref[0] = x_ref[0] + _dot_tn(mixedb_scr[...], wout_ref[...])


def _mlp_kernel(x_ref, g_ref, wup_ref, wdown_ref, gfin_ref, o_ref, *, final_norm):
    x = x_ref[...]
    h = _rmsnorm(x, g_ref[...]).astype(BF16)
    acc = jnp.zeros(x.shape, F32)
    for j in range(D_FF // FF_CHUNK):
        u = _dot(h, wup_ref[:, j * FF_CHUNK:(j + 1) * FF_CHUNK])
        a = jnp.square(jnp.maximum(u, 0.0)).astype(BF16)
        acc = acc + _dot(a, wdown_ref[j * FF_CHUNK:(j + 1) * FF_CHUNK, :])
    y = x + acc
    if final_norm:
        y = _rmsnorm(y, gfin_ref[...])
    o_ref[...] = y


def _resident(shape):
    return pl.BlockSpec(shape, lambda *_: (0,) * len(shape), pipeline_mode=pl.Buffered(1))


def _resident_layer(stacked, layer):
    tail = stacked.shape[1:]
    return pl.BlockSpec((None,) + tail, lambda *_: (layer,) + (0,) * len(tail),
                        pipeline_mode=pl.Buffered(1))


def _mixer_call(x, layer, g, wt, wrq, wk, wat, wbt, wout, sinks, gch, tables, w_up, w_down):
    bsz, seq, d = x.shape
    tm = min(SEQ_TILE, seq)
    feat_tab, tok_tab, band, dmask, zeta, xi = tables
    tok_spec = pl.BlockSpec((tm, tok_tab.shape[1]), lambda b, t: (t, 0))
    feat_spec = pl.BlockSpec((feat_tab.shape[0], tm), lambda b, t: (0, t))
    smem = pl.BlockSpec(memory_space=pltpu.SMEM)
    x_spec = pl.BlockSpec((1, tm, d), lambda b, t: (b, t, 0))
    nt = seq // tm

    def next_tile(b, t):
        flat = jnp.minimum(b * nt + t + 1, bsz * nt - 1)
        return (flat // nt, flat % nt, 0)

    steps = bsz * nt
    n_cast = max(nb for nb in (1, 2, 4, 8, 16) if steps % nb == 0 and nb <= CAST_BLOCKS)
    cast_every = steps // n_cast

    def cast_specs(w):
        rows, cols = w.shape[1] // n_cast, w.shape[2]
        blk = lambda b, t: ((b * nt + t) // cast_every, 0)
        return (pl.BlockSpec((None, rows, cols), lambda b, t: (layer,) + blk(b, t)),
                pl.BlockSpec((rows, cols), blk))

    (wup_in, wup_out), (wdown_in, wdown_out) = cast_specs(w_up), cast_specs(w_down)
    return pl.pallas_call(
        functools.partial(_mixer_kernel, cast_every=cast_every),
        grid=(bsz, nt),
        in_specs=[x_spec, pl.BlockSpec((1, tm, d), next_tile)]
        + [_resident_layer(w, layer) for w in (g, wt, wrq, wk, wat, wbt, wout)]
        + [smem, smem, feat_spec, tok_spec,
           _resident(band.shape), _resident(dmask.shape), _resident(zeta.shape), _resident(xi.shape),
           wup_in, wdown_in],
        out_specs=(x_spec, wup_out, wdown_out),
        out_shape=(jax.ShapeDtypeStruct(x.shape, F32),
                   jax.ShapeDtypeStruct(w_up.shape[1:], BF16),
                   jax.ShapeDtypeStruct(w_down.shape[1:], BF16)),
        scratch_shapes=[
            pltpu.VMEM((tm, d), BF16),
            pltpu.VMEM((d, tm), BF16),
            pltpu.VMEM((N_KV_HEADS, LANES, GQA_GROUP * tm), BF16),
            pltpu.VMEM((tm + ATT_BLOCK, LANES), BF16),
            pltpu.VMEM((ATT_KV, tm + ATT_BLOCK), BF16),
            pltpu.VMEM((ATT_Q, tm), BF16),
            pltpu.VMEM((RET_QK, tm), BF16),
            pltpu.VMEM((tm, RET_QK), BF16),
            pltpu.VMEM((RET_V, tm), F32),
            pltpu.VMEM((RET_V, tm), F32),
            pltpu.VMEM((RET_V, tm), BF16),
            pltpu.VMEM((RET_HEADS, RET_V_DIM, RET_QK_DIM), F32),
            pltpu.VMEM((d, tm), F32),
            pltpu.VMEM((d, tm), F32),
            pltpu.VMEM((d, tm), BF16),
        ],
        compiler_params=pltpu.CompilerParams(
            dimension_semantics=("arbitrary", "arbitrary"),
            vmem_limit_bytes=VMEM_LIMIT_BYTES),
        name="mixer",
    )(x, x, g, wt, wrq, wk, wat, wbt, wout, sinks, gch, feat_tab, tok_tab, band, dmask, zeta, xi,
      w_up, w_down)


def _mlp_call(x2d, layer, g, wup, wdown, gfin, final_norm):
    n, d = x2d.shape
    tm = min(MLP_TILE, n)
    x_spec = pl.BlockSpec((tm, d), lambda i: (i, 0))
    return pl.pallas_call(
        functools.partial(_mlp_kernel, final_norm=final_norm),
        grid=(n // tm,),
        in_specs=[x_spec, _resident_layer(g, layer), _resident(wup.shape), _resident(wdown.shape),
                  _resident((1, d))],
        out_specs=x_spec,
        out_shape=jax.ShapeDtypeStruct(x2d.shape, F32),
        compiler_params=pltpu.CompilerParams(
            dimension_semantics=("arbitrary",),
            vmem_limit_bytes=VMEM_LIMIT_BYTES),
        name="mlp_final" if final_norm else "mlp",
    )(x2d, g, wup, wdown, gfin)


def _even_odd_heads(w):
    lead = w.shape[:-1]
    w = w.reshape(lead + (RET_HEADS, RET_QK_DIM // 2, 2))
    return jnp.swapaxes(w, -1, -2).reshape(lead + (RET_QK,))


def _halves_first(w):
    lead = w.shape[:-1]
    w = w.reshape(lead + (N_KV_HEADS, 2, HALF))
    return jnp.swapaxes(w, -2, -3).reshape(lead + (ATT_KV,))


def _prepare_in_projection(w_in):
    wt = jnp.swapaxes(w_in, 1, 2).astype(BF16)
    wrq = jnp.swapaxes(_even_odd_heads(w_in[:, :, OFF_RQ:OFF_RK]), 1, 2).astype(BF16)
    wk = jnp.concatenate([_halves_first(w_in[:, :, OFF_AK:OFF_AV]),
                          _even_odd_heads(w_in[:, :, OFF_RK:OFF_RV])], axis=2).astype(BF16)
    return wt, wrq, wk


def _as_f32(a):
    return np.ascontiguousarray(a, dtype=np.float32)


@functools.lru_cache(maxsize=None)
def _position_tables(seq):
    pos = np.arange(seq, dtype=np.float64)
    inv = ROPE_THETA ** (-np.arange(HALF, dtype=np.float64) / HALF)
    ang = pos[:, None] * inv[None, :]
    cos, sin = np.cos(ang), np.sin(ang)
    theta = 1.0 / (RET_THETA ** np.linspace(0.0, 1.0, RET_QK_DIM // 2))
    angr = pos[:, None] * theta[None, :]
    cosr, sinr = np.cos(angr), np.sin(angr)
    tok_tab = np.concatenate([cos, cos, cos, cos, -sin, -sin, sin, sin,
                              cosr, cosr, -sinr, sinr], axis=1)
    feat_tab = np.concatenate([cos, sin, cosr, sinr], axis=1).T

    kj = np.arange(2 * ATT_BLOCK)[:, None]
    qi = np.arange(GQA_GROUP * ATT_BLOCK)[None, :] % ATT_BLOCK
    inside = (kj > qi) & (kj <= qi + WINDOW)
    band = np.where(np.stack([inside, inside & (kj >= ATT_BLOCK)]), 0.0, -np.inf)

    log_g = np.log(1.0 - 2.0 ** (-5.0 - np.arange(RET_HEADS, dtype=np.float64)))
    idx = np.arange(RET_BLOCK, dtype=np.float64)
    rel = idx[None, :] - idx[:, None]
    dmask = np.where(rel[None] >= 0, np.exp(log_g[:, None, None] * np.maximum(rel, 0.0)[None]), 0.0)
    zeta = np.exp(log_g[:, None] * (RET_BLOCK - 1.0 - idx)[None, :])
    xi = np.exp(log_g[:, None] * (idx + 1.0)[None, :])
    zeta = np.broadcast_to(zeta[:, None, :], (RET_HEADS, 8, RET_BLOCK))
    xi = np.broadcast_to(xi[:, None, :], (RET_HEADS, 8, RET_BLOCK))
    gch = np.exp(log_g * RET_BLOCK)
    return tuple(_as_f32(a) for a in (feat_tab, tok_tab, band, dmask, zeta, xi)), _as_f32(gch)


def kernel(x, g_mix, w_in, sinks, w_a, w_b, w_out, g_mlp, w_up, w_down, g_final):
    bsz, seq, d = x.shape
    depth = w_in.shape[0]
    tables, gch = _position_tables(seq)
    wt, wrq, wk = _prepare_in_projection(w_in)
    wat = jnp.swapaxes(w_a, 1, 2).astype(BF16)
    wbt = jnp.swapaxes(w_b, 1, 2).astype(BF16)
    wout = w_out.astype(BF16)
    g_mix3 = g_mix[:, None, :]
    g_mlp3 = g_mlp[:, None, :]
    for l in range(depth):
        x, wup, wdown = _mixer_call(x, l, g_mix3, wt, wrq, wk, wat, wbt, wout, sinks[l], gch, tables,
                                    w_up, w_down)
        x = _mlp_call(x.reshape(bsz * seq, d), l, g_mlp3, wup, wdown, g_final[None, :],
                      final_norm=(l == depth - 1)).reshape(bsz, seq, d)
    return x
```

```python
import functools

import numpy as np
import jax
import jax.numpy as jnp
from jax import lax
from jax.experimental import pallas as pl
from jax.experimental.pallas import tpu as pltpu

D_MODEL = 1024
HEAD_DIM = 64
N_Q_HEADS = 8
N_KV_HEADS = 2
GQA_GROUP = N_Q_HEADS // N_KV_HEADS
WINDOW = 128
ROPE_THETA = 10000.0
RET_HEADS = 4
RET_QK_DIM = 128
RET_V_DIM = 2 * RET_QK_DIM
RET_THETA = 10000.0
D_FF = 4 * D_MODEL
EPS = 1e-6
LOG2E = 1.4426950408889634

ATT_Q = N_Q_HEADS * HEAD_DIM
ATT_KV = N_KV_HEADS * HEAD_DIM
RET_QK = RET_HEADS * RET_QK_DIM
RET_V = RET_HEADS * RET_V_DIM

OFF_AQ = 0
OFF_AK = OFF_AQ + ATT_Q
OFF_AV = OFF_AK + ATT_KV
OFF_RQ = OFF_AV + ATT_KV
OFF_RK = OFF_RQ + RET_QK
OFF_RV = OFF_RK + RET_QK
OFF_RG = OFF_RV + RET_V
OFF_GA = OFF_RG + RET_V
OFF_GB = OFF_GA + D_MODEL

LANES = 128
HALF = HEAD_DIM // 2

TOK_CKA, TOK_SKA, TOK_CKR, TOK_SKR = 0, LANES, 2 * LANES, 3 * LANES
FEAT_CQA = 0
FEAT_SQA = FEAT_CQA + HALF
FEAT_CQR = FEAT_SQA + HALF
FEAT_SQR = FEAT_CQR + RET_QK_DIM // 2
FEAT_END = FEAT_SQR + RET_QK_DIM // 2

ATT_BLOCK = WINDOW
RET_BLOCK = 256
SEQ_TILE = 512
ROW_BLOCK = 256
CAST_BLOCKS = 16
MLP_TILE = 1024
FF_CHUNK = 1024
VMEM_LIMIT_BYTES = 56 * 1024 * 1024

BF16 = jnp.bfloat16
F32 = jnp.float32


def _dot(a, b):
    return jnp.dot(a, b, preferred_element_type=F32)


def _dot_tn(a, b):
    return lax.dot_general(a, b, (((0,), (0,)), ((), ())), preferred_element_type=F32)


def _sigmoid(x):
    return 1.0 / (1.0 + jnp.exp2(x * (-LOG2E)))


def _rmsnorm(x, g):
    return x * lax.rsqrt(jnp.mean(x * x, axis=-1, keepdims=True) + EPS) * g


def _mixer_kernel(x_ref, xnext_ref, g_ref, wt_ref, wrq_ref, wk_ref, wat_ref, wbt_ref, wout_ref,
                  sinks_ref, gch_ref, feat_ref, tok_ref, band_ref, dmask_ref, zeta_ref, xi_ref,
                  wup_f32_ref, wdown_f32_ref,
                  o_ref, wup_bf16_ref, wdown_bf16_ref,
                  h_scr, ht_scr, qpad_scr, kband_scr, vt_scr, oatt_scr,
                  rq_scr, rk_scr, rv_scr, silu_scr, rb_scr, state_scr, mixed_scr, sgb_scr, mixedb_scr,
                  *, cast_every):
    tm = x_ref.shape[1]
    t = pl.program_id(1)

    @pl.when(t == 0)
    def _():
        kband_scr[0:ATT_BLOCK, :] = jnp.zeros((ATT_BLOCK, LANES), BF16)
        vt_scr[:, 0:ATT_BLOCK] = jnp.zeros((ATT_KV, ATT_BLOCK), BF16)
        qpad_scr[...] = jnp.zeros(qpad_scr.shape, BF16)
        state_scr[...] = jnp.zeros(state_scr.shape, F32)

    @pl.when((pl.program_id(0) * pl.num_programs(1) + t) % cast_every == 0)
    def _():
        wup_bf16_ref[...] = wup_f32_ref[...].astype(BF16)
        wdown_bf16_ref[...] = wdown_f32_ref[...].astype(BF16)

    def norm_to_scratch(xv):
        hb = _rmsnorm(xv, g_ref[...]).astype(BF16)
        h_scr[...] = hb
        ht_scr[...] = hb.T

    @pl.when((pl.program_id(0) == 0) & (t == 0))
    def _():
        norm_to_scratch(x_ref[0])

    zk = _dot(h_scr[...], wk_ref[...])

    def proj(r0, nrows):
        return _dot(wt_ref[r0:r0 + nrows, :], ht_scr[...])

    kscale = RET_QK_DIM ** -0.5
    hq = RET_QK_DIM // 2

    def item_rk(hd):
        kk = zk[:, ATT_KV + RET_QK_DIM * hd:ATT_KV + RET_QK_DIM * (hd + 1)]
        rk_scr[:, RET_QK_DIM * hd:RET_QK_DIM * (hd + 1)] = (
            (kk * tok_ref[:, TOK_CKR:TOK_CKR + LANES]
             + pltpu.roll(kk, LANES // 2, 1) * tok_ref[:, TOK_SKR:TOK_SKR + LANES]) * kscale).astype(BF16)

    def item_rq(i):
        blk = _dot(wrq_ref[i * ROW_BLOCK:(i + 1) * ROW_BLOCK, :], ht_scr[...])
        cqr = feat_ref[FEAT_CQR:FEAT_SQR, :]
        sqr = feat_ref[FEAT_SQR:FEAT_END, :]
        for hl in range(ROW_BLOCK // RET_QK_DIM):
            base = i * ROW_BLOCK + hl * RET_QK_DIM
            ev = blk[hl * RET_QK_DIM:hl * RET_QK_DIM + hq]
            od = blk[hl * RET_QK_DIM + hq:(hl + 1) * RET_QK_DIM]
            rq_scr[base:base + hq, :] = (ev * cqr - od * sqr).astype(BF16)
            rq_scr[base + hq:base + RET_QK_DIM, :] = (od * cqr + ev * sqr).astype(BF16)

    def item_rv(i):
        rv_scr[i * ROW_BLOCK:(i + 1) * ROW_BLOCK, :] = proj(OFF_RV + i * ROW_BLOCK, ROW_BLOCK)

    def item_rg(i):
        rg = proj(OFF_RG + i * ROW_BLOCK, ROW_BLOCK)
        silu_scr[i * ROW_BLOCK:(i + 1) * ROW_BLOCK, :] = rg * _sigmoid(rg)

    def item_ga(i):
        ga = proj(OFF_GA + i * ROW_BLOCK, ROW_BLOCK)
        ya = _dot(wat_ref[i * ROW_BLOCK:(i + 1) * ROW_BLOCK, :], oatt_scr[...])
        mixed_scr[i * ROW_BLOCK:(i + 1) * ROW_BLOCK, :] = _sigmoid(ga) * ya

    def item_gb(i):
        sgb_scr[i * ROW_BLOCK:(i + 1) * ROW_BLOCK, :] = _sigmoid(proj(OFF_GB + i * ROW_BLOCK, ROW_BLOCK))

    def spread(items, n_units):
        cuts = [round(u * len(items) / n_units) for u in range(n_units + 1)]
        return [items[cuts[u]:cuts[u + 1]] for u in range(n_units)]

    scale = HEAD_DIM ** -0.5 * LOG2E

    def item_aq(j):
        qt = proj(OFF_AQ + j * GQA_GROUP * HEAD_DIM, GQA_GROUP * HEAD_DIM)
        cq = feat_ref[FEAT_CQA:FEAT_SQA, :]
        sq = feat_ref[FEAT_SQA:FEAT_CQR, :]
        for g in range(GQA_GROUP):
            x1 = qt[HEAD_DIM * g:HEAD_DIM * g + HALF]
            x2 = qt[HEAD_DIM * g + HALF:HEAD_DIM * (g + 1)]
            qpad_scr[j, HALF * j:HALF * (j + 1), g * tm:(g + 1) * tm] = (
                (x1 * cq - x2 * sq) * scale).astype(BF16)
            qpad_scr[j, 2 * HALF + HALF * j:2 * HALF + HALF * (j + 1), g * tm:(g + 1) * tm] = (
                (x2 * cq + x1 * sq) * scale).astype(BF16)

    nblk = D_MODEL // ROW_BLOCK
    att_items = ([functools.partial(item_aq, j) for j in range(1, N_KV_HEADS)]
                 + [functools.partial(item_rk, hd) for hd in range(RET_HEADS)]
                 + [functools.partial(item_rq, i) for i in range(RET_QK // ROW_BLOCK)]
                 + [functools.partial(item_rv, i) for i in range(RET_V // ROW_BLOCK)]
                 + [functools.partial(item_rg, i) for i in range(RET_V // ROW_BLOCK)])
    ret_items = ([functools.partial(item_ga, i) for i in range(nblk)]
                 + [functools.partial(item_gb, i) for i in range(nblk)])

    vt_scr[:, ATT_BLOCK:ATT_BLOCK + tm] = proj(OFF_AV, ATT_KV).astype(BF16)
    ka = zk[:, 0:ATT_KV]
    kband_scr[ATT_BLOCK:ATT_BLOCK + tm, :] = (
        ka * tok_ref[:, TOK_CKA:TOK_CKA + LANES]
        + pltpu.roll(ka, LANES // 2, 1) * tok_ref[:, TOK_SKA:TOK_SKA + LANES]).astype(BF16)
    item_aq(0)

    band = band_ref[0]
    band_first = jnp.where(t > 0, band, band_ref[1])
    sink_rows = [
        jnp.concatenate([jnp.full((1, ATT_BLOCK), sinks_ref[GQA_GROUP * j + g] * LOG2E, F32)
                         for g in range(GQA_GROUP)], axis=1) for j in range(N_KV_HEADS)]

    def att_scores(c, j):
        r0 = c * ATT_BLOCK
        qs = jnp.concatenate(
            [qpad_scr[j, :, g * tm + r0:g * tm + r0 + ATT_BLOCK] for g in range(GQA_GROUP)],
            axis=1)
        kb = kband_scr[r0:r0 + 2 * ATT_BLOCK, :]
        return _dot(kb, qs) + (band_first if c == 0 else band)

    def att_finish(c, j, st):
        r0 = c * ATT_BLOCK
        m = jnp.maximum(jnp.max(st, axis=0, keepdims=True), sink_rows[j])
        p = jnp.exp2(st - m)
        denom = jnp.sum(p, axis=0, keepdims=True) + jnp.exp2(sink_rows[j] - m)
        vb = vt_scr[HEAD_DIM * j:HEAD_DIM * (j + 1), r0:r0 + 2 * ATT_BLOCK]
        ot = _dot(vb, p.astype(BF16)) / denom
        for g in range(GQA_GROUP):
            hh = GQA_GROUP * j + g
            oatt_scr[HEAD_DIM * hh:HEAD_DIM * (hh + 1), r0:r0 + ATT_BLOCK] = (
                ot[:, g * ATT_BLOCK:(g + 1) * ATT_BLOCK].astype(BF16))

    att_units = [(c, j) for j in range(N_KV_HEADS) for c in range(tm // ATT_BLOCK)]
    att_sched = spread(att_items, len(att_units))
    st_next = att_scores(*att_units[0])
    for u, unit in enumerate(att_units):
        st_cur = st_next
        if u + 1 < len(att_units):
            st_next = att_scores(*att_units[u + 1])
        for item in att_sched[u]:
            item()
        att_finish(*unit, st_cur)

    kband_scr[0:ATT_BLOCK, :] = kband_scr[tm:tm + ATT_BLOCK, :]
    vt_scr[:, 0:ATT_BLOCK] = vt_scr[:, tm:tm + ATT_BLOCK]

    def ret_scores(c, hd):
        r0 = c * RET_BLOCK
        q = rq_scr[RET_QK_DIM * hd:RET_QK_DIM * (hd + 1), r0:r0 + RET_BLOCK]
        kk = rk_scr[r0:r0 + RET_BLOCK, RET_QK_DIM * hd:RET_QK_DIM * (hd + 1)]
        return (_dot(kk, q) * dmask_ref[hd]).astype(BF16)

    def ret_finish(c, hd, s):
        r0 = c * RET_BLOCK
        q = rq_scr[RET_QK_DIM * hd:RET_QK_DIM * (hd + 1), r0:r0 + RET_BLOCK]
        kk = rk_scr[r0:r0 + RET_BLOCK, RET_QK_DIM * hd:RET_QK_DIM * (hd + 1)]
        vv = rv_scr[RET_V_DIM * hd:RET_V_DIM * (hd + 1), r0:r0 + RET_BLOCK]
        state = state_scr[hd]
        inner = _dot(vv.astype(BF16), s)
        cross = _dot(state.astype(BF16), q) * xi_ref[hd][0:1, :]
        out = inner + cross
        kv = _dot((vv * zeta_ref[hd][0:1, :]).astype(BF16), kk)
        state_scr[hd] = state * gch_ref[hd] + kv
        rn = out * lax.rsqrt(jnp.mean(out * out, axis=0, keepdims=True) + EPS)
        rb_scr[RET_V_DIM * hd:RET_V_DIM * (hd + 1), r0:r0 + RET_BLOCK] = (
            rn * silu_scr[RET_V_DIM * hd:RET_V_DIM * (hd + 1), r0:r0 + RET_BLOCK]).astype(BF16)

    ret_units = [(c, hd) for c in range(tm // RET_BLOCK) for hd in range(RET_HEADS)]
    ret_sched = spread(ret_items, len(ret_units))
    s_next = ret_scores(*ret_units[0])
    for u, unit in enumerate(ret_units):
        s_cur = s_next
        if u + 1 < len(ret_units):
            s_next = ret_scores(*ret_units[u + 1])
        for item in ret_sched[u]:
            item()
        ret_finish(*unit, s_cur)

    norm_to_scratch(xnext_ref[0])
    rb = rb_scr[...]
    for i in range(nblk):
        rows = slice(i * ROW_BLOCK, (i + 1) * ROW_BLOCK)
        yb = _dot(wbt_ref[rows, :], rb)
        mixedb_scr[rows, :] = (mixed_scr[rows, :] + sgb_scr[rows, :] * yb).astype(BF16)
    o_ref[0] = x_ref[0] + _dot_tn(mixedb_scr[...], wout_ref[...])


def _mlp_kernel(x_ref, g_ref, wup_ref, wdown_ref, gfin_ref, o_ref, *, final_norm):
    x = x_ref[...]
    h = _rmsnorm(x, g_ref[...]).astype(BF16)
    acc = jnp.zeros(x.shape, F32)
    for j in range(D_FF // FF_CHUNK):
        u = _dot(h, wup_ref[:, j * FF_CHUNK:(j + 1) * FF_CHUNK])
        a = jnp.square(jnp.maximum(u, 0.0)).astype(BF16)
        acc = acc + _dot(a, wdown_ref[j * FF_CHUNK:(j + 1) * FF_CHUNK, :])
    y = x + acc
    if final_norm:
        y = _rmsnorm(y, gfin_ref[...])
    o_ref[...] = y


def _resident(shape):
    return pl.BlockSpec(shape, lambda *_: (0,) * len(shape), pipeline_mode=pl.Buffered(1))


def _resident_layer(stacked, layer):
    tail = stacked.shape[1:]
    return pl.BlockSpec((None,) + tail, lambda *_: (layer,) + (0,) * len(tail),
                        pipeline_mode=pl.Buffered(1))


def _mixer_call(x, layer, g, wt, wrq, wk, wat, wbt, wout, sinks, gch, tables, w_up, w_down):
    bsz, seq, d = x.shape
    tm = min(SEQ_TILE, seq)
    feat_tab, tok_tab, band, dmask, zeta, xi = tables
    tok_spec = pl.BlockSpec((tm, tok_tab.shape[1]), lambda b, t: (t, 0))
    feat_spec = pl.BlockSpec((feat_tab.shape[0], tm), lambda b, t: (0, t))
    smem = pl.BlockSpec(memory_space=pltpu.SMEM)
    x_spec = pl.BlockSpec((1, tm, d), lambda b, t: (b, t, 0))
    nt = seq // tm

    def next_tile(b, t):
        flat = jnp.minimum(b * nt + t + 1, bsz * nt - 1)
        return (flat // nt, flat % nt, 0)

    steps = bsz * nt
    n_cast = max(nb for nb in (1, 2, 4, 8, 16) if steps % nb == 0 and nb <= CAST_BLOCKS)
    cast_every = steps // n_cast

    def cast_specs(w):
        rows, cols = w.shape[1] // n_cast, w.shape[2]
        blk = lambda b, t: ((b * nt + t) // cast_every, 0)
        return (pl.BlockSpec((None, rows, cols), lambda b, t: (layer,) + blk(b, t)),
                pl.BlockSpec((rows, cols), blk))

    (wup_in, wup_out), (wdown_in, wdown_out) = cast_specs(w_up), cast_specs(w_down)
    return pl.pallas_call(
        functools.partial(_mixer_kernel, cast_every=cast_every),
        grid=(bsz, nt),
        in_specs=[x_spec, pl.BlockSpec((1, tm, d), next_tile)]
        + [_resident_layer(w, layer) for w in (g, wt, wrq, wk, wat, wbt, wout)]
        + [smem, smem, feat_spec, tok_spec,
           _resident(band.shape), _resident(dmask.shape), _resident(zeta.shape), _resident(xi.shape),
           wup_in, wdown_in],
        out_specs=(x_spec, wup_out, wdown_out),
        out_shape=(jax.ShapeDtypeStruct(x.shape, F32),
                   jax.ShapeDtypeStruct(w_up.shape[1:], BF16),
                   jax.ShapeDtypeStruct(w_down.shape[1:], BF16)),
        scratch_shapes=[
            pltpu.VMEM((tm, d), BF16),
            pltpu.VMEM((d, tm), BF16),
            pltpu.VMEM((N_KV_HEADS, LANES, GQA_GROUP * tm), BF16),
            pltpu.VMEM((tm + ATT_BLOCK, LANES), BF16),
            pltpu.VMEM((ATT_KV, tm + ATT_BLOCK), BF16),
            pltpu.VMEM((ATT_Q, tm), BF16),
            pltpu.VMEM((RET_QK, tm), BF16),
            pltpu.VMEM((tm, RET_QK), BF16),
            pltpu.VMEM((RET_V, tm), F32),
            pltpu.VMEM((RET_V, tm), F32),
            pltpu.VMEM((RET_V, tm), BF16),
            pltpu.VMEM((RET_HEADS, RET_V_DIM, RET_QK_DIM), F32),
            pltpu.VMEM((d, tm), F32),
            pltpu.VMEM((d, tm), F32),
            pltpu.VMEM((d, tm), BF16),
        ],
        compiler_params=pltpu.CompilerParams(
            dimension_semantics=("arbitrary", "arbitrary"),
            vmem_limit_bytes=VMEM_LIMIT_BYTES),
        name="mixer",
    )(x, x, g, wt, wrq, wk, wat, wbt, wout, sinks, gch, feat_tab, tok_tab, band, dmask, zeta, xi,
      w_up, w_down)


def _mlp_call(x2d, layer, g, wup, wdown, gfin, final_norm):
    n, d = x2d.shape
    tm = min(MLP_TILE, n)
    x_spec = pl.BlockSpec((tm, d), lambda i: (i, 0))
    return pl.pallas_call(
        functools.partial(_mlp_kernel, final_norm=final_norm),
        grid=(n // tm,),
        in_specs=[x_spec, _resident_layer(g, layer), _resident(wup.shape), _resident(wdown.shape),
                  _resident((1, d))],
        out_specs=x_spec,
        out_shape=jax.ShapeDtypeStruct(x2d.shape, F32),
        compiler_params=pltpu.CompilerParams(
            dimension_semantics=("arbitrary",),
            vmem_limit_bytes=VMEM_LIMIT_BYTES),
        name="mlp_final" if final_norm else "mlp",
    )(x2d, g, wup, wdown, gfin)


def _even_odd_heads(w):
    lead = w.shape[:-1]
    w = w.reshape(lead + (RET_HEADS, RET_QK_DIM // 2, 2))
    return jnp.swapaxes(w, -1, -2).reshape(lead + (RET_QK,))


def _halves_first(w):
    lead = w.shape[:-1]
    w = w.reshape(lead + (N_KV_HEADS, 2, HALF))
    return jnp.swapaxes(w, -2, -3).reshape(lead + (ATT_KV,))


def _prepare_in_projection(w_in):
    wt = jnp.swapaxes(w_in, 1, 2).astype(BF16)
    wrq = jnp.swapaxes(_even_odd_heads(w_in[:, :, OFF_RQ:OFF_RK]), 1, 2).astype(BF16)
    wk = jnp.concatenate([_halves_first(w_in[:, :, OFF_AK:OFF_AV]),
                          _even_odd_heads(w_in[:, :, OFF_RK:OFF_RV])], axis=2).astype(BF16)
    return wt, wrq, wk


def _as_f32(a):
    return np.ascontiguousarray(a, dtype=np.float32)


@functools.lru_cache(maxsize=None)
def _position_tables(seq):
    pos = np.arange(seq, dtype=np.float64)
    inv = ROPE_THETA ** (-np.arange(HALF, dtype=np.float64) / HALF)
    ang = pos[:, None] * inv[None, :]
    cos, sin = np.cos(ang), np.sin(ang)
    theta = 1.0 / (RET_THETA ** np.linspace(0.0, 1.0, RET_QK_DIM // 2))
    angr = pos[:, None] * theta[None, :]
    cosr, sinr = np.cos(angr), np.sin(angr)
    tok_tab = np.concatenate([cos, cos, cos, cos, -sin, -sin, sin, sin,
                              cosr, cosr, -sinr, sinr], axis=1)
    feat_tab = np.concatenate([cos, sin, cosr, sinr], axis=1).T

    kj = np.arange(2 * ATT_BLOCK)[:, None]
    qi = np.arange(GQA_GROUP * ATT_BLOCK)[None, :] % ATT_BLOCK
    inside = (kj > qi) & (kj <= qi + WINDOW)
    band = np.where(np.stack([inside, inside & (kj >= ATT_BLOCK)]), 0.0, -np.inf)

    log_g = np.log(1.0 - 2.0 ** (-5.0 - np.arange(RET_HEADS, dtype=np.float64)))
    idx = np.arange(RET_BLOCK, dtype=np.float64)
    rel = idx[None, :] - idx[:, None]
    dmask = np.where(rel[None] >= 0, np.exp(log_g[:, None, None] * np.maximum(rel, 0.0)[None]), 0.0)
    zeta = np.exp(log_g[:, None] * (RET_BLOCK - 1.0 - idx)[None, :])
    xi = np.exp(log_g[:, None] * (idx + 1.0)[None, :])
    zeta = np.broadcast_to(zeta[:, None, :], (RET_HEADS, 8, RET_BLOCK))
    xi = np.broadcast_to(xi[:, None, :], (RET_HEADS, 8, RET_BLOCK))
    gch = np.exp(log_g * RET_BLOCK)
    return tuple(_as_f32(a) for a in (feat_tab, tok_tab, band, dmask, zeta, xi)), _as_f32(gch)


def kernel(x, g_mix, w_in, sinks, w_a, w_b, w_out, g_mlp, w_up, w_down, g_final):
    bsz, seq, d = x.shape
    depth = w_in.shape[0]
    tables, gch = _position_tables(seq)
    wt, wrq, wk = _prepare_in_projection(w_in)
    wat = jnp.swapaxes(w_a, 1, 2).astype(BF16)
    wbt = jnp.swapaxes(w_b, 1, 2).astype(BF16)
    wout = w_out.astype(BF16)
    g_mix3 = g_mix[:, None, :]
    g_mlp3 = g_mlp[:, None, :]
    for l in range(depth):
        x, wup, wdown = _mixer_call(x, l, g_mix3, wt, wrq, wk, wat, wbt, wout, sinks[l], gch, tables,
                                    w_up, w_down)
        x = _mlp_call(x.reshape(bsz * seq, d), l, g_mlp3, wup, wdown, g_final[None, :],
                      final_norm=(l == depth - 1)).reshape(bsz, seq, d)
    return x
```

```python
import functools

import numpy as np
import jax
import jax.numpy as jnp
from jax import lax
from jax.experimental import pallas as pl
from jax.experimental.pallas import tpu as pltpu

D_MODEL = 1024
HEAD_DIM = 64
N_Q_HEADS = 8
N_KV_HEADS = 2
GQA_GROUP = N_Q_HEADS // N_KV_HEADS
WINDOW = 128
ROPE_THETA = 10000.0
RET_HEADS = 4
RET_QK_DIM = 128
RET_V_DIM = 2 * RET_QK_DIM
RET_THETA = 10000.0
D_FF = 4 * D_MODEL
EPS = 1e-6
LOG2E = 1.4426950408889634

ATT_Q = N_Q_HEADS * HEAD_DIM
ATT_KV = N_KV_HEADS * HEAD_DIM
RET_QK = RET_HEADS * RET_QK_DIM
RET_V = RET_HEADS * RET_V_DIM

OFF_AQ = 0
OFF_AK = OFF_AQ + ATT_Q
OFF_AV = OFF_AK + ATT_KV
OFF_RQ = OFF_AV + ATT_KV
OFF_RK = OFF_RQ + RET_QK
OFF_RV = OFF_RK + RET_QK
OFF_RG = OFF_RV + RET_V
OFF_GA = OFF_RG + RET_V
OFF_GB = OFF_GA + D_MODEL

LANES = 128
HALF = HEAD_DIM // 2

TOK_CKA, TOK_SKA, TOK_CKR, TOK_SKR = 0, LANES, 2 * LANES, 3 * LANES
FEAT_CQA = 0
FEAT_SQA = FEAT_CQA + HALF
FEAT_CQR = FEAT_SQA + HALF
FEAT_SQR = FEAT_CQR + RET_QK_DIM // 2
FEAT_END = FEAT_SQR + RET_QK_DIM // 2

ATT_BLOCK = WINDOW
RET_BLOCK = 256
SEQ_TILE = 512
ROW_BLOCK = 256
CAST_BLOCKS = 16
MLP_TILE = 1024
FF_CHUNK = 1024
VMEM_LIMIT_BYTES = 56 * 1024 * 1024

BF16 = jnp.bfloat16
F32 = jnp.float32


def _dot(a, b):
    return jnp.dot(a, b, preferred_element_type=F32)


def _dot_tn(a, b):
    return lax.dot_general(a, b, (((0,), (0,)), ((), ())), preferred_element_type=F32)


def _sigmoid(x):
    return 1.0 / (1.0 + jnp.exp2(x * (-LOG2E)))


def _rmsnorm(x, g):
    return x * lax.rsqrt(jnp.mean(x * x, axis=-1, keepdims=True) + EPS) * g


def _mixer_kernel(x_ref, xnext_ref, g_ref, wt_ref, wrq_ref, wk_ref, wat_ref, wbt_ref, wout_ref,
                  sinks_ref, gch_ref, feat_ref, tok_ref, band_ref, dmask_ref, zeta_ref, xi_ref,
                  wup_f32_ref, wdown_f32_ref,
                  o_ref, wup_bf16_ref, wdown_bf16_ref,
                  h_scr, ht_scr, qpad_scr, kband_scr, vt_scr, oatt_scr,
                  rq_scr, rk_scr, rv_scr, silu_scr, rb_scr, state_scr, mixed_scr, sgb_scr, mixedb_scr,
                  *, cast_every):
    tm = x_ref.shape[1]
    t = pl.program_id(1)

    @pl.when(t == 0)
    def _():
        kband_scr[0:ATT_BLOCK, :] = jnp.zeros((ATT_BLOCK, LANES), BF16)
        vt_scr[:, 0:ATT_BLOCK] = jnp.zeros((ATT_KV, ATT_BLOCK), BF16)
        qpad_scr[...] = jnp.zeros(qpad_scr.shape, BF16)
        state_scr[...] = jnp.zeros(state_scr.shape, F32)

    @pl.when((pl.program_id(0) * pl.num_programs(1) + t) % cast_every == 0)
    def _():
        wup_bf16_ref[...] = wup_f32_ref[...].astype(BF16)
        wdown_bf16_ref[...] = wdown_f32_ref[...].astype(BF16)

    def norm_to_scratch(xv):
        hb = _rmsnorm(xv, g_ref[...]).astype(BF16)
        h_scr[...] = hb
        ht_scr[...] = hb.T

    @pl.when((pl.program_id(0) == 0) & (t == 0))
    def _():
        norm_to_scratch(x_ref[0])

    zk = _dot(h_scr[...], wk_ref[...])

    def proj(r0, nrows):
        return _dot(wt_ref[r0:r0 + nrows, :], ht_scr[...])

    kscale = RET_QK_DIM ** -0.5
    hq = RET_QK_DIM // 2

    def item_rk(hd):
        kk = zk[:, ATT_KV + RET_QK_DIM * hd:ATT_KV + RET_QK_DIM * (hd + 1)]
        rk_scr[:, RET_QK_DIM * hd:RET_QK_DIM * (hd + 1)] = (
            (kk * tok_ref[:, TOK_CKR:TOK_CKR + LANES]
             + pltpu.roll(kk, LANES // 2, 1) * tok_ref[:, TOK_SKR:TOK_SKR + LANES]) * kscale).astype(BF16)

    def item_rq(i):
        blk = _dot(wrq_ref[i * ROW_BLOCK:(i + 1) * ROW_BLOCK, :], ht_scr[...])
        cqr = feat_ref[FEAT_CQR:FEAT_SQR, :]
        sqr = feat_ref[FEAT_SQR:FEAT_END, :]
        for hl in range(ROW_BLOCK // RET_QK_DIM):
            base = i * ROW_BLOCK + hl * RET_QK_DIM
            ev = blk[hl * RET_QK_DIM:hl * RET_QK_DIM + hq]
            od = blk[hl * RET_QK_DIM + hq:(hl + 1) * RET_QK_DIM]
            rq_scr[base:base + hq, :] = (ev * cqr - od * sqr).astype(BF16)
            rq_scr[base + hq:base + RET_QK_DIM, :] = (od * cqr + ev * sqr).astype(BF16)

    def item_rv(i):
        rv_scr[i * ROW_BLOCK:(i + 1) * ROW_BLOCK, :] = proj(OFF_RV + i * ROW_BLOCK, ROW_BLOCK)

    def item_rg(i):
        rg = proj(OFF_RG + i * ROW_BLOCK, ROW_BLOCK)
        silu_scr[i * ROW_BLOCK:(i + 1) * ROW_BLOCK, :] = rg * _sigmoid(rg)

    def item_ga(i):
        ga = proj(OFF_GA + i * ROW_BLOCK, ROW_BLOCK)
        ya = _dot(wat_ref[i * ROW_BLOCK:(i + 1) * ROW_BLOCK, :], oatt_scr[...])
        mixed_scr[i * ROW_BLOCK:(i + 1) * ROW_BLOCK, :] = _sigmoid(ga) * ya

    def item_gb(i):
        sgb_scr[i * ROW_BLOCK:(i + 1) * ROW_BLOCK, :] = _sigmoid(proj(OFF_GB + i * ROW_BLOCK, ROW_BLOCK))

    def spread(items, n_units):
        cuts = [round(u * len(items) / n_units) for u in range(n_units + 1)]
        return [items[cuts[u]:cuts[u + 1]] for u in range(n_units)]

    scale = HEAD_DIM ** -0.5 * LOG2E

    def item_aq(j):
        qt = proj(OFF_AQ + j * GQA_GROUP * HEAD_DIM, GQA_GROUP * HEAD_DIM)
        cq = feat_ref[FEAT_CQA:FEAT_SQA, :]
        sq = feat_ref[FEAT_SQA:FEAT_CQR, :]
        for g in range(GQA_GROUP):
            x1 = qt[HEAD_DIM * g:HEAD_DIM * g + HALF]
            x2 = qt[HEAD_DIM * g + HALF:HEAD_DIM * (g + 1)]
            qpad_scr[j, HALF * j:HALF * (j + 1), g * tm:(g + 1) * tm] = (
                (x1 * cq - x2 * sq) * scale).astype(BF16)
            qpad_scr[j, 2 * HALF + HALF * j:2 * HALF + HALF * (j + 1), g * tm:(g + 1) * tm] = (
                (x2 * cq + x1 * sq) * scale).astype(BF16)

    nblk = D_MODEL // ROW_BLOCK
    att_items = ([functools.partial(item_aq, j) for j in range(1, N_KV_HEADS)]
                 + [functools.partial(item_rk, hd) for hd in range(RET_HEADS)]
                 + [functools.partial(item_rq, i) for i in range(RET_QK // ROW_BLOCK)]
                 + [functools.partial(item_rv, i) for i in range(RET_V // ROW_BLOCK)]
                 + [functools.partial(item_rg, i) for i in range(RET_V // ROW_BLOCK)])
    ret_items = ([functools.partial(item_ga, i) for i in range(nblk)]
                 + [functools.partial(item_gb, i) for i in range(nblk)])

    vt_scr[:, ATT_BLOCK:ATT_BLOCK + tm] = proj(OFF_AV, ATT_KV).astype(BF16)
    ka = zk[:, 0:ATT_KV]
    kband_scr[ATT_BLOCK:ATT_BLOCK + tm, :] = (
        ka * tok_ref[:, TOK_CKA:TOK_CKA + LANES]
        + pltpu.roll(ka, LANES // 2, 1) * tok_ref[:, TOK_SKA:TOK_SKA + LANES]).astype(BF16)
    item_aq(0)

    band = band_ref[0]
    band_first = jnp.where(t > 0, band, band_ref[1])
    sink_rows = [
        jnp.concatenate([jnp.full((1, ATT_BLOCK), sinks_ref[GQA_GROUP * j + g] * LOG2E, F32)
                         for g in range(GQA_GROUP)], axis=1) for j in range(N_KV_HEADS)]

    def att_scores(c, j):
        r0 = c * ATT_BLOCK
        qs = jnp.concatenate(
            [qpad_scr[j, :, g * tm + r0:g * tm + r0 + ATT_BLOCK] for g in range(GQA_GROUP)],
            axis=1)
        kb = kband_scr[r0:r0 + 2 * ATT_BLOCK, :]
        return _dot(kb, qs) + (band_first if c == 0 else band)

    def att_finish(c, j, st):
        r0 = c * ATT_BLOCK
        m = jnp.maximum(jnp.max(st, axis=0, keepdims=True), sink_rows[j])
        p = jnp.exp2(st - m)
        denom = jnp.sum(p, axis=0, keepdims=True) + jnp.exp2(sink_rows[j] - m)
        vb = vt_scr[HEAD_DIM * j:HEAD_DIM * (j + 1), r0:r0 + 2 * ATT_BLOCK]
        ot = _dot(vb, p.astype(BF16)) / denom
        for g in range(GQA_GROUP):
            hh = GQA_GROUP * j + g
            oatt_scr[HEAD_DIM * hh:HEAD_DIM * (hh + 1), r0:r0 + ATT_BLOCK] = (
                ot[:, g * ATT_BLOCK:(g + 1) * ATT_BLOCK].astype(BF16))

    att_units = [(c, j) for j in range(N_KV_HEADS) for c in range(tm // ATT_BLOCK)]
    att_sched = spread(att_items, len(att_units))
    st_next = att_scores(*att_units[0])
    for u, unit in enumerate(att_units):
        st_cur = st_next
        if u + 1 < len(att_units):
            st_next = att_scores(*att_units[u + 1])
        for item in att_sched[u]:
            item()
        att_finish(*unit, st_cur)

    kband_scr[0:ATT_BLOCK, :] = kband_scr[tm:tm + ATT_BLOCK, :]
    vt_scr[:, 0:ATT_BLOCK] = vt_scr[:, tm:tm + ATT_BLOCK]

    def ret_scores(c, hd):
        r0 = c * RET_BLOCK
        q = rq_scr[RET_QK_DIM * hd:RET_QK_DIM * (hd + 1), r0:r0 + RET_BLOCK]
        kk = rk_scr[r0:r0 + RET_BLOCK, RET_QK_DIM * hd:RET_QK_DIM * (hd + 1)]
        return (_dot(kk, q) * dmask_ref[hd]).astype(BF16)

    def ret_finish(c, hd, s):
        r0 = c * RET_BLOCK
        q = rq_scr[RET_QK_DIM * hd:RET_QK_DIM * (hd + 1), r0:r0 + RET_BLOCK]
        kk = rk_scr[r0:r0 + RET_BLOCK, RET_QK_DIM * hd:RET_QK_DIM * (hd + 1)]
        vv = rv_scr[RET_V_DIM * hd:RET_V_DIM * (hd + 1), r0:r0 + RET_BLOCK]
        state = state_scr[hd]
        inner = _dot(vv.astype(BF16), s)
        cross = _dot(state.astype(BF16), q) * xi_ref[hd][0:1, :]
        out = inner + cross
        kv = _dot((vv * zeta_ref[hd][0:1, :]).astype(BF16), kk)
        state_scr[hd] = state * gch_ref[hd] + kv
        rn = out * lax.rsqrt(jnp.mean(out * out, axis=0, keepdims=True) + EPS)
        rb_scr[RET_V_DIM * hd:RET_V_DIM * (hd + 1), r0:r0 + RET_BLOCK] = (
            rn * silu_scr[RET_V_DIM * hd:RET_V_DIM * (hd + 1), r0:r0 + RET_BLOCK]).astype(BF16)

    ret_units = [(c, hd) for c in range(tm // RET_BLOCK) for hd in range(RET_HEADS)]
    ret_sched = spread(ret_items, len(ret_units))
    s_next = ret_scores(*ret_units[0])
    for u, unit in enumerate(ret_units):
        s_cur = s_next
        if u + 1 < len(ret_units):
            s_next = ret_scores(*ret_units[u + 1])
        for item in ret_sched[u]:
            item()
        ret_finish(*unit, s_cur)

    norm_to_scratch(xnext_ref[0])
    rb = rb_scr[...]
    for i in range(nblk):
        rows = slice(i * ROW_BLOCK, (i + 1) * ROW_BLOCK)
        yb = _dot(wbt_ref[rows, :], rb)
        mixedb_scr[rows, :] = (mixed_scr[rows, :] + sgb_scr[rows, :] * yb).astype(BF16)
    o_ref[0] = x_ref[0] + _dot_tn(mixedb_scr[...], wout_ref[...])


def _mlp_kernel(x_ref, g_ref, wup_ref, wdown_ref, gfin_ref, *rest, final_norm, n_prep):
    if n_prep:
        win_next_ref, o_ref, wt_next_ref = rest

        @pl.when(pl.program_id(0) < n_prep)
        def _():
            wt_next_ref[...] = win_next_ref[...].astype(BF16).T
    else:
        (o_ref,) = rest

    x = x_ref[...]
    h = _rmsnorm(x, g_ref[...]).astype(BF16)
    acc = jnp.zeros(x.shape, F32)
    for j in range(D_FF // FF_CHUNK):
        u = _dot(h, wup_ref[:, j * FF_CHUNK:(j + 1) * FF_CHUNK])
        a = jnp.square(jnp.maximum(u, 0.0)).astype(BF16)
        acc = acc + _dot(a, wdown_ref[j * FF_CHUNK:(j + 1) * FF_CHUNK, :])
    y = x + acc
    if final_norm:
        y = _rmsnorm(y, gfin_ref[...])
    o_ref[...] = y


def _resident(shape):
    return pl.BlockSpec(shape, lambda *_: (0,) * len(shape), pipeline_mode=pl.Buffered(1))


def _resident_layer(stacked, layer):
    tail = stacked.shape[1:]
    return pl.BlockSpec((None,) + tail, lambda *_: (layer,) + (0,) * len(tail),
                        pipeline_mode=pl.Buffered(1))


def _mixer_call(x, layer, g, wt, wrq, wk, wat, wbt, wout, sinks, gch, tables, w_up, w_down):
    bsz, seq, d = x.shape
    tm = min(SEQ_TILE, seq)
    feat_tab, tok_tab, band, dmask, zeta, xi = tables
    tok_spec = pl.BlockSpec((tm, tok_tab.shape[1]), lambda b, t: (t, 0))
    feat_spec = pl.BlockSpec((feat_tab.shape[0], tm), lambda b, t: (0, t))
    smem = pl.BlockSpec(memory_space=pltpu.SMEM)
    x_spec = pl.BlockSpec((1, tm, d), lambda b, t: (b, t, 0))
    nt = seq // tm

    def next_tile(b, t):
        flat = jnp.minimum(b * nt + t + 1, bsz * nt - 1)
        return (flat // nt, flat % nt, 0)

    steps = bsz * nt
    n_cast = max(nb for nb in (1, 2, 4, 8, 16) if steps % nb == 0 and nb <= CAST_BLOCKS)
    cast_every = steps // n_cast

    def cast_specs(w):
        rows, cols = w.shape[1] // n_cast, w.shape[2]
        blk = lambda b, t: ((b * nt + t) // cast_every, 0)
        return (pl.BlockSpec((None, rows, cols), lambda b, t: (layer,) + blk(b, t)),
                pl.BlockSpec((rows, cols), blk))

    (wup_in, wup_out), (wdown_in, wdown_out) = cast_specs(w_up), cast_specs(w_down)
    return pl.pallas_call(
        functools.partial(_mixer_kernel, cast_every=cast_every),
        grid=(bsz, nt),
        in_specs=[x_spec, pl.BlockSpec((1, tm, d), next_tile)]
        + [_resident_layer(g, layer), _resident(wt.shape)]
        + [_resident_layer(w, layer) for w in (wrq, wk, wat, wbt, wout)]
        + [smem, smem, feat_spec, tok_spec,
           _resident(band.shape), _resident(dmask.shape), _resident(zeta.shape), _resident(xi.shape),
           wup_in, wdown_in],
        out_specs=(x_spec, wup_out, wdown_out),
        out_shape=(jax.ShapeDtypeStruct(x.shape, F32),
                   jax.ShapeDtypeStruct(w_up.shape[1:], BF16),
                   jax.ShapeDtypeStruct(w_down.shape[1:], BF16)),
        scratch_shapes=[
            pltpu.VMEM((tm, d), BF16),
            pltpu.VMEM((d, tm), BF16),
            pltpu.VMEM((N_KV_HEADS, LANES, GQA_GROUP * tm), BF16),
            pltpu.VMEM((tm + ATT_BLOCK, LANES), BF16),
            pltpu.VMEM((ATT_KV, tm + ATT_BLOCK), BF16),
            pltpu.VMEM((ATT_Q, tm), BF16),
            pltpu.VMEM((RET_QK, tm), BF16),
            pltpu.VMEM((tm, RET_QK), BF16),
            pltpu.VMEM((RET_V, tm), F32),
            pltpu.VMEM((RET_V, tm), F32),
            pltpu.VMEM((RET_V, tm), BF16),
            pltpu.VMEM((RET_HEADS, RET_V_DIM, RET_QK_DIM), F32),
            pltpu.VMEM((d, tm), F32),
            pltpu.VMEM((d, tm), F32),
            pltpu.VMEM((d, tm), BF16),
        ],
        compiler_params=pltpu.CompilerParams(
            dimension_semantics=("arbitrary", "arbitrary"),
            vmem_limit_bytes=VMEM_LIMIT_BYTES),
        name="mixer",
    )(x, x, g, wt, wrq, wk, wat, wbt, wout, sinks, gch, feat_tab, tok_tab, band, dmask, zeta, xi,
      w_up, w_down)


def _mlp_call(x2d, layer, g, wup, wdown, gfin, final_norm, w_in_next=None):
    n, d = x2d.shape
    tm = min(MLP_TILE, n)
    steps = n // tm
    x_spec = pl.BlockSpec((tm, d), lambda i: (i, 0))
    in_specs = [x_spec, _resident_layer(g, layer), _resident(wup.shape), _resident(wdown.shape),
                _resident((1, d))]
    out_specs, out_shape, operands = x_spec, jax.ShapeDtypeStruct(x2d.shape, F32), (x2d, g, wup, wdown, gfin)
    n_prep = 0
    if w_in_next is not None:
        w_cols = w_in_next.shape[2]
        n_prep = max(nb for nb in range(1, steps + 1) if w_cols % (nb * LANES) == 0)
        cols = w_cols // n_prep
        blk = lambda i: jnp.minimum(i, n_prep - 1)
        in_specs.append(pl.BlockSpec((None, d, cols), lambda i: (layer + 1, 0, blk(i))))
        out_specs = (x_spec, pl.BlockSpec((cols, d), lambda i: (blk(i), 0)))
        out_shape = (out_shape, jax.ShapeDtypeStruct((w_cols, d), BF16))
        operands += (w_in_next,)
    return pl.pallas_call(
        functools.partial(_mlp_kernel, final_norm=final_norm, n_prep=n_prep),
        grid=(steps,),
        in_specs=in_specs,
        out_specs=out_specs,
        out_shape=out_shape,
        compiler_params=pltpu.CompilerParams(
            dimension_semantics=("arbitrary",),
            vmem_limit_bytes=VMEM_LIMIT_BYTES),
        name="mlp_final" if final_norm else "mlp",
    )(*operands)


def _even_odd_heads(w):
    lead = w.shape[:-1]
    w = w.reshape(lead + (RET_HEADS, RET_QK_DIM // 2, 2))
    return jnp.swapaxes(w, -1, -2).reshape(lead + (RET_QK,))


def _halves_first(w):
    lead = w.shape[:-1]
    w = w.reshape(lead + (N_KV_HEADS, 2, HALF))
    return jnp.swapaxes(w, -2, -3).reshape(lead + (ATT_KV,))


def _prepare_in_projection(w_in):
    wt = jnp.swapaxes(w_in[0], 0, 1).astype(BF16)
    wrq = jnp.swapaxes(_even_odd_heads(w_in[:, :, OFF_RQ:OFF_RK]), 1, 2).astype(BF16)
    wk = jnp.concatenate([_halves_first(w_in[:, :, OFF_AK:OFF_AV]),
                          _even_odd_heads(w_in[:, :, OFF_RK:OFF_RV])], axis=2).astype(BF16)
    return wt, wrq, wk


def _as_f32(a):
    return np.ascontiguousarray(a, dtype=np.float32)


@functools.lru_cache(maxsize=None)
def _position_tables(seq):
    pos = np.arange(seq, dtype=np.float64)
    inv = ROPE_THETA ** (-np.arange(HALF, dtype=np.float64) / HALF)
    ang = pos[:, None] * inv[None, :]
    cos, sin = np.cos(ang), np.sin(ang)
    theta = 1.0 / (RET_THETA ** np.linspace(0.0, 1.0, RET_QK_DIM // 2))
    angr = pos[:, None] * theta[None, :]
    cosr, sinr = np.cos(angr), np.sin(angr)
    tok_tab = np.concatenate([cos, cos, cos, cos, -sin, -sin, sin, sin,
                              cosr, cosr, -sinr, sinr], axis=1)
    feat_tab = np.concatenate([cos, sin, cosr, sinr], axis=1).T

    kj = np.arange(2 * ATT_BLOCK)[:, None]
    qi = np.arange(GQA_GROUP * ATT_BLOCK)[None, :] % ATT_BLOCK
    inside = (kj > qi) & (kj <= qi + WINDOW)
    band = np.where(np.stack([inside, inside & (kj >= ATT_BLOCK)]), 0.0, -np.inf)

    log_g = np.log(1.0 - 2.0 ** (-5.0 - np.arange(RET_HEADS, dtype=np.float64)))
    idx = np.arange(RET_BLOCK, dtype=np.float64)
    rel = idx[None, :] - idx[:, None]
    dmask = np.where(rel[None] >= 0, np.exp(log_g[:, None, None] * np.maximum(rel, 0.0)[None]), 0.0)
    zeta = np.exp(log_g[:, None] * (RET_BLOCK - 1.0 - idx)[None, :])
    xi = np.exp(log_g[:, None] * (idx + 1.0)[None, :])
    zeta = np.broadcast_to(zeta[:, None, :], (RET_HEADS, 8, RET_BLOCK))
    xi = np.broadcast_to(xi[:, None, :], (RET_HEADS, 8, RET_BLOCK))
    gch = np.exp(log_g * RET_BLOCK)
    return tuple(_as_f32(a) for a in (feat_tab, tok_tab, band, dmask, zeta, xi)), _as_f32(gch)


def kernel(x, g_mix, w_in, sinks, w_a, w_b, w_out, g_mlp, w_up, w_down, g_final):
    bsz, seq, d = x.shape
    depth = w_in.shape[0]
    tables, gch = _position_tables(seq)
    wt, wrq, wk = _prepare_in_projection(w_in)
    wat = jnp.swapaxes(w_a, 1, 2).astype(BF16)
    wbt = jnp.swapaxes(w_b, 1, 2).astype(BF16)
    wout = w_out.astype(BF16)
    g_mix3 = g_mix[:, None, :]
    g_mlp3 = g_mlp[:, None, :]
    for l in range(depth):
        x, wup, wdown = _mixer_call(x, l, g_mix3, wt, wrq, wk, wat, wbt, wout, sinks[l], gch, tables,
                                    w_up, w_down)
        last = l == depth - 1
        out = _mlp_call(x.reshape(bsz * seq, d), l, g_mlp3, wup, wdown, g_final[None, :],
                        final_norm=last, w_in_next=None if last else w_in)
        x, wt = (out, None) if last else out
        x = x.reshape(bsz, seq, d)
    return x
```

```python
import functools

import numpy as np
import jax
import jax.numpy as jnp
from jax import lax
from jax.experimental import pallas as pl
from jax.experimental.pallas import tpu as pltpu

D_MODEL = 1024
HEAD_DIM = 64
N_Q_HEADS = 8
N_KV_HEADS = 2
GQA_GROUP = N_Q_HEADS // N_KV_HEADS
WINDOW = 128
ROPE_THETA = 10000.0
RET_HEADS = 4
RET_QK_DIM = 128
RET_V_DIM = 2 * RET_QK_DIM
RET_THETA = 10000.0
D_FF = 4 * D_MODEL
EPS = 1e-6
LOG2E = 1.4426950408889634

ATT_Q = N_Q_HEADS * HEAD_DIM
ATT_KV = N_KV_HEADS * HEAD_DIM
RET_QK = RET_HEADS * RET_QK_DIM
RET_V = RET_HEADS * RET_V_DIM

OFF_AQ = 0
OFF_AK = OFF_AQ + ATT_Q
OFF_AV = OFF_AK + ATT_KV
OFF_RQ = OFF_AV + ATT_KV
OFF_RK = OFF_RQ + RET_QK
OFF_RV = OFF_RK + RET_QK
OFF_RG = OFF_RV + RET_V
OFF_GA = OFF_RG + RET_V
OFF_GB = OFF_GA + D_MODEL

LANES = 128
HALF = HEAD_DIM // 2

TOK_CKA, TOK_SKA, TOK_CKR, TOK_SKR = 0, LANES, 2 * LANES, 3 * LANES
FEAT_CQA = 0
FEAT_SQA = FEAT_CQA + HALF
FEAT_CQR = FEAT_SQA + HALF
FEAT_SQR = FEAT_CQR + RET_QK_DIM // 2
FEAT_END = FEAT_SQR + RET_QK_DIM // 2

ATT_BLOCK = WINDOW
RET_BLOCK = 256
SEQ_TILE = 512
ROW_BLOCK = 256
CAST_BLOCKS = 16
MLP_TILE = 1024
FF_CHUNK = 1024
VMEM_LIMIT_BYTES = 56 * 1024 * 1024

BF16 = jnp.bfloat16
F32 = jnp.float32


def _dot(a, b):
    return jnp.dot(a, b, preferred_element_type=F32)


def _dot_tn(a, b):
    return lax.dot_general(a, b, (((0,), (0,)), ((), ())), preferred_element_type=F32)


def _sigmoid(x):
    return 1.0 / (1.0 + jnp.exp2(x * (-LOG2E)))


def _rmsnorm(x, g):
    return x * lax.rsqrt(jnp.mean(x * x, axis=-1, keepdims=True) + EPS) * g


def _mixer_kernel(x_ref, xnext_ref, g_ref, wt_ref, wrq_ref, wk_ref, wat_ref, wbt_ref, wout_ref,
                  sinks_ref, gch_ref, feat_ref, tok_ref, band_ref, dmask_ref, zeta_ref, xi_ref,
                  wup_f32_ref, wdown_f32_ref,
                  o_ref, wup_bf16_ref, wdown_bf16_ref,
                  h_scr, ht_scr, qpad_scr, kband_scr, vt_scr, oatt_scr,
                  rq_scr, rqx_scr, rk_scr, rv_scr, rvz_scr, silu_scr, rb_scr, state_scr, mixed_scr, sgb_scr, mixedb_scr,
                  *, cast_every):
    tm = x_ref.shape[1]
    t = pl.program_id(1)

    @pl.when(t == 0)
    def _():
        kband_scr[0:ATT_BLOCK, :] = jnp.zeros((ATT_BLOCK, LANES), BF16)
        vt_scr[:, 0:ATT_BLOCK] = jnp.zeros((ATT_KV, ATT_BLOCK), BF16)
        qpad_scr[...] = jnp.zeros(qpad_scr.shape, BF16)
        state_scr[...] = jnp.zeros(state_scr.shape, F32)

    @pl.when((pl.program_id(0) * pl.num_programs(1) + t) % cast_every == 0)
    def _():
        wup_bf16_ref[...] = wup_f32_ref[...].astype(BF16)
        wdown_bf16_ref[...] = wdown_f32_ref[...].astype(BF16)

    def norm_to_scratch(xv):
        hb = _rmsnorm(xv, g_ref[...]).astype(BF16)
        h_scr[...] = hb
        ht_scr[...] = hb.T

    @pl.when((pl.program_id(0) == 0) & (t == 0))
    def _():
        norm_to_scratch(x_ref[0])

    zk = _dot(h_scr[...], wk_ref[...])

    def proj(r0, nrows):
        return _dot(wt_ref[r0:r0 + nrows, :], ht_scr[...])

    kscale = RET_QK_DIM ** -0.5
    hq = RET_QK_DIM // 2

    def item_rk(hd):
        kk = zk[:, ATT_KV + RET_QK_DIM * hd:ATT_KV + RET_QK_DIM * (hd + 1)]
        rk_scr[:, RET_QK_DIM * hd:RET_QK_DIM * (hd + 1)] = (
            (kk * tok_ref[:, TOK_CKR:TOK_CKR + LANES]
             + pltpu.roll(kk, LANES // 2, 1) * tok_ref[:, TOK_SKR:TOK_SKR + LANES]) * kscale).astype(BF16)

    def item_rq(i):
        blk = _dot(wrq_ref[i * ROW_BLOCK:(i + 1) * ROW_BLOCK, :], ht_scr[...])
        cqr = feat_ref[FEAT_CQR:FEAT_SQR, :]
        sqr = feat_ref[FEAT_SQR:FEAT_END, :]
        heads_per_item = ROW_BLOCK // RET_QK_DIM
        for hl in range(heads_per_item):
            base = i * ROW_BLOCK + hl * RET_QK_DIM
            ev = blk[hl * RET_QK_DIM:hl * RET_QK_DIM + hq]
            od = blk[hl * RET_QK_DIM + hq:(hl + 1) * RET_QK_DIM]
            q_ev = ev * cqr - od * sqr
            q_od = od * cqr + ev * sqr
            rq_scr[base:base + hq, :] = q_ev.astype(BF16)
            rq_scr[base + hq:base + RET_QK_DIM, :] = q_od.astype(BF16)
            xi = jnp.concatenate([xi_ref[i * heads_per_item + hl][0:1, :]] * (tm // RET_BLOCK), axis=1)
            rqx_scr[base:base + hq, :] = (q_ev * xi).astype(BF16)
            rqx_scr[base + hq:base + RET_QK_DIM, :] = (q_od * xi).astype(BF16)

    def item_rv(i):
        rows = slice(i * ROW_BLOCK, (i + 1) * ROW_BLOCK)
        vt = proj(OFF_RV + i * ROW_BLOCK, ROW_BLOCK)
        zeta = jnp.concatenate([zeta_ref[i * ROW_BLOCK // RET_V_DIM][0:1, :]] * (tm // RET_BLOCK), axis=1)
        rv_scr[rows, :] = vt.astype(BF16)
        rvz_scr[rows, :] = (vt * zeta).astype(BF16)

    def item_rg(i):
        rg = proj(OFF_RG + i * ROW_BLOCK, ROW_BLOCK)
        silu_scr[i * ROW_BLOCK:(i + 1) * ROW_BLOCK, :] = rg * _sigmoid(rg)

    def item_ga(i):
        ga = proj(OFF_GA + i * ROW_BLOCK, ROW_BLOCK)
        ya = _dot(wat_ref[i * ROW_BLOCK:(i + 1) * ROW_BLOCK, :], oatt_scr[...])
        mixed_scr[i * ROW_BLOCK:(i + 1) * ROW_BLOCK, :] = _sigmoid(ga) * ya

    def item_gb(i):
        sgb_scr[i * ROW_BLOCK:(i + 1) * ROW_BLOCK, :] = _sigmoid(proj(OFF_GB + i * ROW_BLOCK, ROW_BLOCK))

    def spread(items, n_units):
        cuts = [round(u * len(items) / n_units) for u in range(n_units + 1)]
        return [items[cuts[u]:cuts[u + 1]] for u in range(n_units)]

    scale = HEAD_DIM ** -0.5 * LOG2E

    def item_aq(j):
        qt = proj(OFF_AQ + j * GQA_GROUP * HEAD_DIM, GQA_GROUP * HEAD_DIM)
        cq = feat_ref[FEAT_CQA:FEAT_SQA, :]
        sq = feat_ref[FEAT_SQA:FEAT_CQR, :]
        for g in range(GQA_GROUP):
            x1 = qt[HEAD_DIM * g:HEAD_DIM * g + HALF]
            x2 = qt[HEAD_DIM * g + HALF:HEAD_DIM * (g + 1)]
            qpad_scr[j, HALF * j:HALF * (j + 1), g * tm:(g + 1) * tm] = (
                (x1 * cq - x2 * sq) * scale).astype(BF16)
            qpad_scr[j, 2 * HALF + HALF * j:2 * HALF + HALF * (j + 1), g * tm:(g + 1) * tm] = (
                (x2 * cq + x1 * sq) * scale).astype(BF16)

    nblk = D_MODEL // ROW_BLOCK
    att_items = ([functools.partial(item_aq, j) for j in range(1, N_KV_HEADS)]
                 + [functools.partial(item_rk, hd) for hd in range(RET_HEADS)]
                 + [functools.partial(item_rq, i) for i in range(RET_QK // ROW_BLOCK)]
                 + [functools.partial(item_rv, i) for i in range(RET_V // ROW_BLOCK)]
                 + [functools.partial(item_rg, i) for i in range(RET_V // ROW_BLOCK)])
    ret_items = ([functools.partial(item_ga, i) for i in range(nblk)]
                 + [functools.partial(item_gb, i) for i in range(nblk)])

    vt_scr[:, ATT_BLOCK:ATT_BLOCK + tm] = proj(OFF_AV, ATT_KV).astype(BF16)
    ka = zk[:, 0:ATT_KV]
    kband_scr[ATT_BLOCK:ATT_BLOCK + tm, :] = (
        ka * tok_ref[:, TOK_CKA:TOK_CKA + LANES]
        + pltpu.roll(ka, LANES // 2, 1) * tok_ref[:, TOK_SKA:TOK_SKA + LANES]).astype(BF16)
    item_aq(0)

    band = band_ref[0]
    band_first = jnp.where(t > 0, band, band_ref[1])
    sink_rows = [
        jnp.concatenate([jnp.full((1, ATT_BLOCK), sinks_ref[GQA_GROUP * j + g] * LOG2E, F32)
                         for g in range(GQA_GROUP)], axis=1) for j in range(N_KV_HEADS)]

    def att_scores(c, j):
        r0 = c * ATT_BLOCK
        qs = jnp.concatenate(
            [qpad_scr[j, :, g * tm + r0:g * tm + r0 + ATT_BLOCK] for g in range(GQA_GROUP)],
            axis=1)
        kb = kband_scr[r0:r0 + 2 * ATT_BLOCK, :]
        return _dot(kb, qs) + (band_first if c == 0 else band)

    def att_finish(c, j, st):
        r0 = c * ATT_BLOCK
        m = jnp.maximum(jnp.max(st, axis=0, keepdims=True), sink_rows[j])
        p = jnp.exp2(st - m)
        denom = jnp.sum(p, axis=0, keepdims=True) + jnp.exp2(sink_rows[j] - m)
        vb = vt_scr[HEAD_DIM * j:HEAD_DIM * (j + 1), r0:r0 + 2 * ATT_BLOCK]
        ot = _dot(vb, p.astype(BF16)) / denom
        for g in range(GQA_GROUP):
            hh = GQA_GROUP * j + g
            oatt_scr[HEAD_DIM * hh:HEAD_DIM * (hh + 1), r0:r0 + ATT_BLOCK] = (
                ot[:, g * ATT_BLOCK:(g + 1) * ATT_BLOCK].astype(BF16))

    att_units = [(c, j) for j in range(N_KV_HEADS) for c in range(tm // ATT_BLOCK)]
    att_sched = spread(att_items, len(att_units))
    st_next = att_scores(*att_units[0])
    for u, unit in enumerate(att_units):
        st_cur = st_next
        if u + 1 < len(att_units):
            st_next = att_scores(*att_units[u + 1])
        for item in att_sched[u]:
            item()
        att_finish(*unit, st_cur)

    kband_scr[0:ATT_BLOCK, :] = kband_scr[tm:tm + ATT_BLOCK, :]
    vt_scr[:, 0:ATT_BLOCK] = vt_scr[:, tm:tm + ATT_BLOCK]

    def ret_scores(c, hd):
        r0 = c * RET_BLOCK
        q = rq_scr[RET_QK_DIM * hd:RET_QK_DIM * (hd + 1), r0:r0 + RET_BLOCK]
        kk = rk_scr[r0:r0 + RET_BLOCK, RET_QK_DIM * hd:RET_QK_DIM * (hd + 1)]
        return (_dot(kk, q) * dmask_ref[hd]).astype(BF16)

    def ret_finish(c, hd, s):
        r0 = c * RET_BLOCK
        qx = rqx_scr[RET_QK_DIM * hd:RET_QK_DIM * (hd + 1), r0:r0 + RET_BLOCK]
        kk = rk_scr[r0:r0 + RET_BLOCK, RET_QK_DIM * hd:RET_QK_DIM * (hd + 1)]
        vv = rv_scr[RET_V_DIM * hd:RET_V_DIM * (hd + 1), r0:r0 + RET_BLOCK]
        vz = rvz_scr[RET_V_DIM * hd:RET_V_DIM * (hd + 1), r0:r0 + RET_BLOCK]
        state = state_scr[hd]
        out = _dot(jnp.concatenate([vv, state.astype(BF16)], axis=1),
                   jnp.concatenate([s, qx], axis=0))
        kv = _dot(vz, kk)
        state_scr[hd] = state * gch_ref[hd] + kv
        rn = out * lax.rsqrt(jnp.mean(out * out, axis=0, keepdims=True) + EPS)
        rb_scr[RET_V_DIM * hd:RET_V_DIM * (hd + 1), r0:r0 + RET_BLOCK] = (
            rn * silu_scr[RET_V_DIM * hd:RET_V_DIM * (hd + 1), r0:r0 + RET_BLOCK]).astype(BF16)

    ret_units = [(c, hd) for c in range(tm // RET_BLOCK) for hd in range(RET_HEADS)]
    ret_sched = spread(ret_items, len(ret_units))
    s_next = ret_scores(*ret_units[0])
    for u, unit in enumerate(ret_units):
        s_cur = s_next
        if u + 1 < len(ret_units):
            s_next = ret_scores(*ret_units[u + 1])
        for item in ret_sched[u]:
            item()
        ret_finish(*unit, s_cur)

    norm_to_scratch(xnext_ref[0])
    rb = rb_scr[...]
    for i in range(nblk):
        rows = slice(i * ROW_BLOCK, (i + 1) * ROW_BLOCK)
        yb = _dot(wbt_ref[rows, :], rb)
        mixedb_scr[rows, :] = (mixed_scr[rows, :] + sgb_scr[rows, :] * yb).astype(BF16)
    o_ref[0] = x_ref[0] + _dot_tn(mixedb_scr[...], wout_ref[...])


def _mlp_kernel(x_ref, g_ref, wup_ref, wdown_ref, gfin_ref, o_ref, *, final_norm):
    x = x_ref[...]
    h = _rmsnorm(x, g_ref[...]).astype(BF16)
    acc = jnp.zeros(x.shape, F32)
    for j in range(D_FF // FF_CHUNK):
        u = _dot(h, wup_ref[:, j * FF_CHUNK:(j + 1) * FF_CHUNK])
        a = jnp.square(jnp.maximum(u, 0.0)).astype(BF16)
        acc = acc + _dot(a, wdown_ref[j * FF_CHUNK:(j + 1) * FF_CHUNK, :])
    y = x + acc
    if final_norm:
        y = _rmsnorm(y, gfin_ref[...])
    o_ref[...] = y


def _resident(shape):
    return pl.BlockSpec(shape, lambda *_: (0,) * len(shape), pipeline_mode=pl.Buffered(1))


def _resident_layer(stacked, layer):
    tail = stacked.shape[1:]
    return pl.BlockSpec((None,) + tail, lambda *_: (layer,) + (0,) * len(tail),
                        pipeline_mode=pl.Buffered(1))


def _mixer_call(x, layer, g, wt, wrq, wk, wat, wbt, wout, sinks, gch, tables, w_up, w_down):
    bsz, seq, d = x.shape
    tm = min(SEQ_TILE, seq)
    feat_tab, tok_tab, band, dmask, zeta, xi = tables
    tok_spec = pl.BlockSpec((tm, tok_tab.shape[1]), lambda b, t: (t, 0))
    feat_spec = pl.BlockSpec((feat_tab.shape[0], tm), lambda b, t: (0, t))
    smem = pl.BlockSpec(memory_space=pltpu.SMEM)
    x_spec = pl.BlockSpec((1, tm, d), lambda b, t: (b, t, 0))
    nt = seq // tm

    def next_tile(b, t):
        flat = jnp.minimum(b * nt + t + 1, bsz * nt - 1)
        return (flat // nt, flat % nt, 0)

    steps = bsz * nt
    n_cast = max(nb for nb in (1, 2, 4, 8, 16) if steps % nb == 0 and nb <= CAST_BLOCKS)
    cast_every = steps // n_cast

    def cast_specs(w):
        rows, cols = w.shape[1] // n_cast, w.shape[2]
        blk = lambda b, t: ((b * nt + t) // cast_every, 0)
        return (pl.BlockSpec((None, rows, cols), lambda b, t: (layer,) + blk(b, t)),
                pl.BlockSpec((rows, cols), blk))

    (wup_in, wup_out), (wdown_in, wdown_out) = cast_specs(w_up), cast_specs(w_down)
    return pl.pallas_call(
        functools.partial(_mixer_kernel, cast_every=cast_every),
        grid=(bsz, nt),
        in_specs=[x_spec, pl.BlockSpec((1, tm, d), next_tile)]
        + [_resident_layer(w, layer) for w in (g, wt, wrq, wk, wat, wbt, wout)]
        + [smem, smem, feat_spec, tok_spec,
           _resident(band.shape), _resident(dmask.shape), _resident(zeta.shape), _resident(xi.shape),
           wup_in, wdown_in],
        out_specs=(x_spec, wup_out, wdown_out),
        out_shape=(jax.ShapeDtypeStruct(x.shape, F32),
                   jax.ShapeDtypeStruct(w_up.shape[1:], BF16),
                   jax.ShapeDtypeStruct(w_down.shape[1:], BF16)),
        scratch_shapes=[
            pltpu.VMEM((tm, d), BF16),
            pltpu.VMEM((d, tm), BF16),
            pltpu.VMEM((N_KV_HEADS, LANES, GQA_GROUP * tm), BF16),
            pltpu.VMEM((tm + ATT_BLOCK, LANES), BF16),
            pltpu.VMEM((ATT_KV, tm + ATT_BLOCK), BF16),
            pltpu.VMEM((ATT_Q, tm), BF16),
            pltpu.VMEM((RET_QK, tm), BF16),
            pltpu.VMEM((RET_QK, tm), BF16),
            pltpu.VMEM((tm, RET_QK), BF16),
            pltpu.VMEM((RET_V, tm), BF16),
            pltpu.VMEM((RET_V, tm), BF16),
            pltpu.VMEM((RET_V, tm), F32),
            pltpu.VMEM((RET_V, tm), BF16),
            pltpu.VMEM((RET_HEADS, RET_V_DIM, RET_QK_DIM), F32),
            pltpu.VMEM((d, tm), F32),
            pltpu.VMEM((d, tm), F32),
            pltpu.VMEM((d, tm), BF16),
        ],
        compiler_params=pltpu.CompilerParams(
            dimension_semantics=("arbitrary", "arbitrary"),
            vmem_limit_bytes=VMEM_LIMIT_BYTES),
        name="mixer",
    )(x, x, g, wt, wrq, wk, wat, wbt, wout, sinks, gch, feat_tab, tok_tab, band, dmask, zeta, xi,
      w_up, w_down)


def _mlp_call(x2d, layer, g, wup, wdown, gfin, final_norm):
    n, d = x2d.shape
    tm = min(MLP_TILE, n)
    x_spec = pl.BlockSpec((tm, d), lambda i: (i, 0))
    return pl.pallas_call(
        functools.partial(_mlp_kernel, final_norm=final_norm),
        grid=(n // tm,),
        in_specs=[x_spec, _resident_layer(g, layer), _resident(wup.shape), _resident(wdown.shape),
                  _resident((1, d))],
        out_specs=x_spec,
        out_shape=jax.ShapeDtypeStruct(x2d.shape, F32),
        compiler_params=pltpu.CompilerParams(
            dimension_semantics=("arbitrary",),
            vmem_limit_bytes=VMEM_LIMIT_BYTES),
        name="mlp_final" if final_norm else "mlp",
    )(x2d, g, wup, wdown, gfin)


def _even_odd_heads(w):
    lead = w.shape[:-1]
    w = w.reshape(lead + (RET_HEADS, RET_QK_DIM // 2, 2))
    return jnp.swapaxes(w, -1, -2).reshape(lead + (RET_QK,))


def _halves_first(w):
    lead = w.shape[:-1]
    w = w.reshape(lead + (N_KV_HEADS, 2, HALF))
    return jnp.swapaxes(w, -2, -3).reshape(lead + (ATT_KV,))


def _prepare_in_projection(w_in):
    wt = jnp.swapaxes(w_in, 1, 2).astype(BF16)
    wrq = jnp.swapaxes(_even_odd_heads(w_in[:, :, OFF_RQ:OFF_RK]), 1, 2).astype(BF16)
    wk = jnp.concatenate([_halves_first(w_in[:, :, OFF_AK:OFF_AV]),
                          _even_odd_heads(w_in[:, :, OFF_RK:OFF_RV])], axis=2).astype(BF16)
    return wt, wrq, wk


def _as_f32(a):
    return np.ascontiguousarray(a, dtype=np.float32)


@functools.lru_cache(maxsize=None)
def _position_tables(seq):
    pos = np.arange(seq, dtype=np.float64)
    inv = ROPE_THETA ** (-np.arange(HALF, dtype=np.float64) / HALF)
    ang = pos[:, None] * inv[None, :]
    cos, sin = np.cos(ang), np.sin(ang)
    theta = 1.0 / (RET_THETA ** np.linspace(0.0, 1.0, RET_QK_DIM // 2))
    angr = pos[:, None] * theta[None, :]
    cosr, sinr = np.cos(angr), np.sin(angr)
    tok_tab = np.concatenate([cos, cos, cos, cos, -sin, -sin, sin, sin,
                              cosr, cosr, -sinr, sinr], axis=1)
    feat_tab = np.concatenate([cos, sin, cosr, sinr], axis=1).T

    kj = np.arange(2 * ATT_BLOCK)[:, None]
    qi = np.arange(GQA_GROUP * ATT_BLOCK)[None, :] % ATT_BLOCK
    inside = (kj > qi) & (kj <= qi + WINDOW)
    band = np.where(np.stack([inside, inside & (kj >= ATT_BLOCK)]), 0.0, -np.inf)

    log_g = np.log(1.0 - 2.0 ** (-5.0 - np.arange(RET_HEADS, dtype=np.float64)))
    idx = np.arange(RET_BLOCK, dtype=np.float64)
    rel = idx[None, :] - idx[:, None]
    dmask = np.where(rel[None] >= 0, np.exp(log_g[:, None, None] * np.maximum(rel, 0.0)[None]), 0.0)
    zeta = np.exp(log_g[:, None] * (RET_BLOCK - 1.0 - idx)[None, :])
    xi = np.exp(log_g[:, None] * (idx + 1.0)[None, :])
    zeta = np.broadcast_to(zeta[:, None, :], (RET_HEADS, 8, RET_BLOCK))
    xi = np.broadcast_to(xi[:, None, :], (RET_HEADS, 8, RET_BLOCK))
    gch = np.exp(log_g * RET_BLOCK)
    return tuple(_as_f32(a) for a in (feat_tab, tok_tab, band, dmask, zeta, xi)), _as_f32(gch)


def kernel(x, g_mix, w_in, sinks, w_a, w_b, w_out, g_mlp, w_up, w_down, g_final):
    bsz, seq, d = x.shape
    depth = w_in.shape[0]
    tables, gch = _position_tables(seq)
    wt, wrq, wk = _prepare_in_projection(w_in)
    wat = jnp.swapaxes(w_a, 1, 2).astype(BF16)
    wbt = jnp.swapaxes(w_b, 1, 2).astype(BF16)
    wout = w_out.astype(BF16)
    g_mix3 = g_mix[:, None, :]
    g_mlp3 = g_mlp[:, None, :]
    for l in range(depth):
        x, wup, wdown = _mixer_call(x, l, g_mix3, wt, wrq, wk, wat, wbt, wout, sinks[l], gch, tables,
                                    w_up, w_down)
        x = _mlp_call(x.reshape(bsz * seq, d), l, g_mlp3, wup, wdown, g_final[None, :],
                      final_norm=(l == depth - 1)).reshape(bsz, seq, d)
    return x
```

```python
import functools

import numpy as np
import jax
import jax.numpy as jnp
from jax import lax
from jax.experimental import pallas as pl
from jax.experimental.pallas import tpu as pltpu

D_MODEL = 1024
HEAD_DIM = 64
N_Q_HEADS = 8
N_KV_HEADS = 2
GQA_GROUP = N_Q_HEADS // N_KV_HEADS
WINDOW = 128
ROPE_THETA = 10000.0
RET_HEADS = 4
RET_QK_DIM = 128
RET_V_DIM = 2 * RET_QK_DIM
RET_THETA = 10000.0
D_FF = 4 * D_MODEL
EPS = 1e-6
LOG2E = 1.4426950408889634

ATT_Q = N_Q_HEADS * HEAD_DIM
ATT_KV = N_KV_HEADS * HEAD_DIM
RET_QK = RET_HEADS * RET_QK_DIM
RET_V = RET_HEADS * RET_V_DIM

OFF_AQ = 0
OFF_AK = OFF_AQ + ATT_Q
OFF_AV = OFF_AK + ATT_KV
OFF_RQ = OFF_AV + ATT_KV
OFF_RK = OFF_RQ + RET_QK
OFF_RV = OFF_RK + RET_QK
OFF_RG = OFF_RV + RET_V
OFF_GA = OFF_RG + RET_V
OFF_GB = OFF_GA + D_MODEL

LANES = 128
HALF = HEAD_DIM // 2

TOK_CKA, TOK_SKA, TOK_CKR, TOK_SKR = 0, LANES, 2 * LANES, 3 * LANES
FEAT_CQA = 0
FEAT_SQA = FEAT_CQA + HALF
FEAT_CQR = FEAT_SQA + HALF
FEAT_SQR = FEAT_CQR + RET_QK_DIM // 2
FEAT_END = FEAT_SQR + RET_QK_DIM // 2

ATT_BLOCK = WINDOW
RET_BLOCK = 256
SEQ_TILE = 512
ROW_BLOCK = 256
CAST_BLOCKS = 16
MLP_TILE = 1024
FF_CHUNK = 1024
VMEM_LIMIT_BYTES = 56 * 1024 * 1024

BF16 = jnp.bfloat16
F32 = jnp.float32


def _dot(a, b):
    return jnp.dot(a, b, preferred_element_type=F32)


def _dot_tn(a, b):
    return lax.dot_general(a, b, (((0,), (0,)), ((), ())), preferred_element_type=F32)


def _sigmoid(x):
    return 1.0 / (1.0 + jnp.exp2(x * (-LOG2E)))


def _rmsnorm(x, g):
    return x * lax.rsqrt(jnp.mean(x * x, axis=-1, keepdims=True) + EPS) * g


def _mixer_kernel(x_ref, xnext_ref, g_ref, wt_ref, wrq_ref, wk_ref, wat_ref, wbt_ref, wout_ref,
                  sinks_ref, gch_ref, feat_ref, tok_ref, band_ref, dmask_ref, zeta_ref, xi_ref,
                  wup_f32_ref, wdown_f32_ref,
                  o_ref, wup_bf16_ref, wdown_bf16_ref,
                  h_scr, ht_scr, qpad_scr, kband_scr, vt_scr, oatt_scr,
                  rq_scr, rqx_scr, rk_scr, rv_scr, rvz_scr, silu_scr, rb_scr, state_scr, mixed_scr, sgb_scr, mixedb_scr,
                  *, cast_every):
    tm = x_ref.shape[1]
    t = pl.program_id(1)

    @pl.when(t == 0)
    def _():
        kband_scr[0:ATT_BLOCK, :] = jnp.zeros((ATT_BLOCK, LANES), BF16)
        vt_scr[:, 0:ATT_BLOCK] = jnp.zeros((ATT_KV, ATT_BLOCK), BF16)
        qpad_scr[...] = jnp.zeros(qpad_scr.shape, BF16)
        state_scr[...] = jnp.zeros(state_scr.shape, F32)

    @pl.when((pl.program_id(0) * pl.num_programs(1) + t) % cast_every == 0)
    def _():
        wup_bf16_ref[...] = wup_f32_ref[...].astype(BF16)
        wdown_bf16_ref[...] = wdown_f32_ref[...].astype(BF16)

    def norm_to_scratch(xv):
        hb = _rmsnorm(xv, g_ref[...]).astype(BF16)
        h_scr[...] = hb
        ht_scr[...] = hb.T

    @pl.when((pl.program_id(0) == 0) & (t == 0))
    def _():
        norm_to_scratch(x_ref[0])

    zk = _dot(h_scr[...], wk_ref[...])

    def proj(r0, nrows):
        return _dot(wt_ref[r0:r0 + nrows, :], ht_scr[...])

    hq = RET_QK_DIM // 2

    def item_rk(hd):
        kk = zk[:, ATT_KV + RET_QK_DIM * hd:ATT_KV + RET_QK_DIM * (hd + 1)]
        rk_scr[:, RET_QK_DIM * hd:RET_QK_DIM * (hd + 1)] = (
            kk * tok_ref[:, TOK_CKR:TOK_CKR + LANES]
            + pltpu.roll(kk, LANES // 2, 1) * tok_ref[:, TOK_SKR:TOK_SKR + LANES]).astype(BF16)

    def item_rq(i):
        blk = _dot(wrq_ref[i * ROW_BLOCK:(i + 1) * ROW_BLOCK, :], ht_scr[...])
        cqr = feat_ref[FEAT_CQR:FEAT_SQR, :]
        sqr = feat_ref[FEAT_SQR:FEAT_END, :]
        heads_per_item = ROW_BLOCK // RET_QK_DIM
        for hl in range(heads_per_item):
            base = i * ROW_BLOCK + hl * RET_QK_DIM
            ev = blk[hl * RET_QK_DIM:hl * RET_QK_DIM + hq]
            od = blk[hl * RET_QK_DIM + hq:(hl + 1) * RET_QK_DIM]
            q_ev = ev * cqr - od * sqr
            q_od = od * cqr + ev * sqr
            rq_scr[base:base + hq, :] = q_ev.astype(BF16)
            rq_scr[base + hq:base + RET_QK_DIM, :] = q_od.astype(BF16)
            xi = jnp.concatenate([xi_ref[i * heads_per_item + hl][0:1, :]] * (tm // RET_BLOCK), axis=1)
            rqx_scr[base:base + hq, :] = (q_ev * xi).astype(BF16)
            rqx_scr[base + hq:base + RET_QK_DIM, :] = (q_od * xi).astype(BF16)

    def item_rv(i):
        rows = slice(i * ROW_BLOCK, (i + 1) * ROW_BLOCK)
        vt = proj(OFF_RV + i * ROW_BLOCK, ROW_BLOCK)
        zeta = jnp.concatenate([zeta_ref[i * ROW_BLOCK // RET_V_DIM][0:1, :]] * (tm // RET_BLOCK), axis=1)
        rv_scr[rows, :] = vt.astype(BF16)
        rvz_scr[rows, :] = (vt * zeta).astype(BF16)

    def item_rg(i):
        rg = proj(OFF_RG + i * ROW_BLOCK, ROW_BLOCK)
        silu_scr[i * ROW_BLOCK:(i + 1) * ROW_BLOCK, :] = rg * _sigmoid(rg)

    def item_ga(i):
        ga = proj(OFF_GA + i * ROW_BLOCK, ROW_BLOCK)
        ya = _dot(wat_ref[i * ROW_BLOCK:(i + 1) * ROW_BLOCK, :], oatt_scr[...])
        mixed_scr[i * ROW_BLOCK:(i + 1) * ROW_BLOCK, :] = _sigmoid(ga) * ya

    def item_gb(i):
        sgb_scr[i * ROW_BLOCK:(i + 1) * ROW_BLOCK, :] = _sigmoid(proj(OFF_GB + i * ROW_BLOCK, ROW_BLOCK))

    def spread(items, n_units):
        cuts = [round(u * len(items) / n_units) for u in range(n_units + 1)]
        return [items[cuts[u]:cuts[u + 1]] for u in range(n_units)]

    def item_aq(j):
        qt = proj(OFF_AQ + j * GQA_GROUP * HEAD_DIM, GQA_GROUP * HEAD_DIM)
        cq = feat_ref[FEAT_CQA:FEAT_SQA, :]
        sq = feat_ref[FEAT_SQA:FEAT_CQR, :]
        for g in range(GQA_GROUP):
            x1 = qt[HEAD_DIM * g:HEAD_DIM * g + HALF]
            x2 = qt[HEAD_DIM * g + HALF:HEAD_DIM * (g + 1)]
            qpad_scr[j, HALF * j:HALF * (j + 1), g * tm:(g + 1) * tm] = (
                x1 * cq - x2 * sq).astype(BF16)
            qpad_scr[j, 2 * HALF + HALF * j:2 * HALF + HALF * (j + 1), g * tm:(g + 1) * tm] = (
                x2 * cq + x1 * sq).astype(BF16)

    nblk = D_MODEL // ROW_BLOCK
    att_items = ([functools.partial(item_aq, j) for j in range(1, N_KV_HEADS)]
                 + [functools.partial(item_rk, hd) for hd in range(RET_HEADS)]
                 + [functools.partial(item_rq, i) for i in range(RET_QK // ROW_BLOCK)]
                 + [functools.partial(item_rv, i) for i in range(RET_V // ROW_BLOCK)]
                 + [functools.partial(item_rg, i) for i in range(RET_V // ROW_BLOCK)])
    ret_items = ([functools.partial(item_ga, i) for i in range(nblk)]
                 + [functools.partial(item_gb, i) for i in range(nblk)])

    vt_scr[:, ATT_BLOCK:ATT_BLOCK + tm] = proj(OFF_AV, ATT_KV).astype(BF16)
    ka = zk[:, 0:ATT_KV]
    kband_scr[ATT_BLOCK:ATT_BLOCK + tm, :] = (
        ka * tok_ref[:, TOK_CKA:TOK_CKA + LANES]
        + pltpu.roll(ka, LANES // 2, 1) * tok_ref[:, TOK_SKA:TOK_SKA + LANES]).astype(BF16)
    item_aq(0)

    band = band_ref[0]
    band_first = jnp.where(t > 0, band, band_ref[1])
    sink_rows = [
        jnp.concatenate([jnp.full((1, ATT_BLOCK), sinks_ref[GQA_GROUP * j + g] * LOG2E, F32)
                         for g in range(GQA_GROUP)], axis=1) for j in range(N_KV_HEADS)]

    def att_scores(c, j):
        r0 = c * ATT_BLOCK
        qs = jnp.concatenate(
            [qpad_scr[j, :, g * tm + r0:g * tm + r0 + ATT_BLOCK] for g in range(GQA_GROUP)],
            axis=1)
        kb = kband_scr[r0:r0 + 2 * ATT_BLOCK, :]
        return _dot(kb, qs) + (band_first if c == 0 else band)

    def att_finish(c, j, st):
        r0 = c * ATT_BLOCK
        m = jnp.maximum(jnp.max(st, axis=0, keepdims=True), sink_rows[j])
        p = jnp.exp2(st - m)
        denom = jnp.sum(p, axis=0, keepdims=True) + jnp.exp2(sink_rows[j] - m)
        vb = vt_scr[HEAD_DIM * j:HEAD_DIM * (j + 1), r0:r0 + 2 * ATT_BLOCK]
        ot = _dot(vb, p.astype(BF16)) * (1.0 / denom)
        for g in range(GQA_GROUP):
            hh = GQA_GROUP * j + g
            oatt_scr[HEAD_DIM * hh:HEAD_DIM * (hh + 1), r0:r0 + ATT_BLOCK] = (
                ot[:, g * ATT_BLOCK:(g + 1) * ATT_BLOCK].astype(BF16))

    att_units = [(c, j) for j in range(N_KV_HEADS) for c in range(tm // ATT_BLOCK)]
    att_sched = spread(att_items, len(att_units))
    st_next = att_scores(*att_units[0])
    for u, unit in enumerate(att_units):
        st_cur = st_next
        if u + 1 < len(att_units):
            st_next = att_scores(*att_units[u + 1])
        for item in att_sched[u]:
            item()
        att_finish(*unit, st_cur)

    kband_scr[0:ATT_BLOCK, :] = kband_scr[tm:tm + ATT_BLOCK, :]
    vt_scr[:, 0:ATT_BLOCK] = vt_scr[:, tm:tm + ATT_BLOCK]

    def ret_scores(c, hd):
        r0 = c * RET_BLOCK
        q = rq_scr[RET_QK_DIM * hd:RET_QK_DIM * (hd + 1), r0:r0 + RET_BLOCK]
        kk = rk_scr[r0:r0 + RET_BLOCK, RET_QK_DIM * hd:RET_QK_DIM * (hd + 1)]
        return (_dot(kk, q) * dmask_ref[hd]).astype(BF16)

    def ret_finish(c, hd, s):
        r0 = c * RET_BLOCK
        qx = rqx_scr[RET_QK_DIM * hd:RET_QK_DIM * (hd + 1), r0:r0 + RET_BLOCK]
        kk = rk_scr[r0:r0 + RET_BLOCK, RET_QK_DIM * hd:RET_QK_DIM * (hd + 1)]
        vv = rv_scr[RET_V_DIM * hd:RET_V_DIM * (hd + 1), r0:r0 + RET_BLOCK]
        vz = rvz_scr[RET_V_DIM * hd:RET_V_DIM * (hd + 1), r0:r0 + RET_BLOCK]
        state = state_scr[hd]
        out = _dot(jnp.concatenate([vv, state.astype(BF16)], axis=1),
                   jnp.concatenate([s, qx], axis=0))
        kv = _dot(vz, kk)
        state_scr[hd] = state * gch_ref[hd] + kv
        rn = out * lax.rsqrt(jnp.mean(out * out, axis=0, keepdims=True) + EPS)
        rb_scr[RET_V_DIM * hd:RET_V_DIM * (hd + 1), r0:r0 + RET_BLOCK] = (
            rn * silu_scr[RET_V_DIM * hd:RET_V_DIM * (hd + 1), r0:r0 + RET_BLOCK]).astype(BF16)

    ret_units = [(c, hd) for c in range(tm // RET_BLOCK) for hd in range(RET_HEADS)]
    ret_sched = spread(ret_items, len(ret_units))
    s_next = ret_scores(*ret_units[0])
    for u, unit in enumerate(ret_units):
        s_cur = s_next
        if u + 1 < len(ret_units):
            s_next = ret_scores(*ret_units[u + 1])
        for item in ret_sched[u]:
            item()
        ret_finish(*unit, s_cur)

    norm_to_scratch(xnext_ref[0])
    rb = rb_scr[...]
    for i in range(nblk):
        rows = slice(i * ROW_BLOCK, (i + 1) * ROW_BLOCK)
        yb = _dot(wbt_ref[rows, :], rb)
        mixedb_scr[rows, :] = (mixed_scr[rows, :] + sgb_scr[rows, :] * yb).astype(BF16)
    o_ref[0] = x_ref[0] + _dot_tn(mixedb_scr[...], wout_ref[...])


def _mlp_kernel(x_ref, g_ref, wup_ref, wdown_ref, gfin_ref, o_ref, *, final_norm):
    x = x_ref[...]
    h = _rmsnorm(x, g_ref[...]).astype(BF16)
    acc = jnp.zeros(x.shape, F32)
    for j in range(D_FF // FF_CHUNK):
        u = _dot(h, wup_ref[:, j * FF_CHUNK:(j + 1) * FF_CHUNK])
        a = jnp.square(jnp.maximum(u, 0.0)).astype(BF16)
        acc = acc + _dot(a, wdown_ref[j * FF_CHUNK:(j + 1) * FF_CHUNK, :])
    y = x + acc
    if final_norm:
        y = _rmsnorm(y, gfin_ref[...])
    o_ref[...] = y


def _resident(shape):
    return pl.BlockSpec(shape, lambda *_: (0,) * len(shape), pipeline_mode=pl.Buffered(1))


def _resident_layer(stacked, layer):
    tail = stacked.shape[1:]
    return pl.BlockSpec((None,) + tail, lambda *_: (layer,) + (0,) * len(tail),
                        pipeline_mode=pl.Buffered(1))


def _mixer_call(x, layer, g, wt, wrq, wk, wat, wbt, wout, sinks, gch, tables, w_up, w_down):
    bsz, seq, d = x.shape
    tm = min(SEQ_TILE, seq)
    feat_tab, tok_tab, band, dmask, zeta, xi = tables
    tok_spec = pl.BlockSpec((tm, tok_tab.shape[1]), lambda b, t: (t, 0))
    feat_spec = pl.BlockSpec((feat_tab.shape[0], tm), lambda b, t: (0, t))
    smem = pl.BlockSpec(memory_space=pltpu.SMEM)
    x_spec = pl.BlockSpec((1, tm, d), lambda b, t: (b, t, 0))
    nt = seq // tm

    def next_tile(b, t):
        flat = jnp.minimum(b * nt + t + 1, bsz * nt - 1)
        return (flat // nt, flat % nt, 0)

    steps = bsz * nt
    n_cast = max(nb for nb in (1, 2, 4, 8, 16) if steps % nb == 0 and nb <= CAST_BLOCKS)
    cast_every = steps // n_cast

    def cast_specs(w):
        rows, cols = w.shape[1] // n_cast, w.shape[2]
        blk = lambda b, t: ((b * nt + t) // cast_every, 0)
        return (pl.BlockSpec((None, rows, cols), lambda b, t: (layer,) + blk(b, t)),
                pl.BlockSpec((rows, cols), blk))

    (wup_in, wup_out), (wdown_in, wdown_out) = cast_specs(w_up), cast_specs(w_down)
    return pl.pallas_call(
        functools.partial(_mixer_kernel, cast_every=cast_every),
        grid=(bsz, nt),
        in_specs=[x_spec, pl.BlockSpec((1, tm, d), next_tile)]
        + [_resident_layer(w, layer) for w in (g, wt, wrq, wk, wat, wbt, wout)]
        + [smem, smem, feat_spec, tok_spec,
           _resident(band.shape), _resident(dmask.shape), _resident(zeta.shape), _resident(xi.shape),
           wup_in, wdown_in],
        out_specs=(x_spec, wup_out, wdown_out),
        out_shape=(jax.ShapeDtypeStruct(x.shape, F32),
                   jax.ShapeDtypeStruct(w_up.shape[1:], BF16),
                   jax.ShapeDtypeStruct(w_down.shape[1:], BF16)),
        scratch_shapes=[
            pltpu.VMEM((tm, d), BF16),
            pltpu.VMEM((d, tm), BF16),
            pltpu.VMEM((N_KV_HEADS, LANES, GQA_GROUP * tm), BF16),
            pltpu.VMEM((tm + ATT_BLOCK, LANES), BF16),
            pltpu.VMEM((ATT_KV, tm + ATT_BLOCK), BF16),
            pltpu.VMEM((ATT_Q, tm), BF16),
            pltpu.VMEM((RET_QK, tm), BF16),
            pltpu.VMEM((RET_QK, tm), BF16),
            pltpu.VMEM((tm, RET_QK), BF16),
            pltpu.VMEM((RET_V, tm), BF16),
            pltpu.VMEM((RET_V, tm), BF16),
            pltpu.VMEM((RET_V, tm), F32),
            pltpu.VMEM((RET_V, tm), BF16),
            pltpu.VMEM((RET_HEADS, RET_V_DIM, RET_QK_DIM), F32),
            pltpu.VMEM((d, tm), F32),
            pltpu.VMEM((d, tm), F32),
            pltpu.VMEM((d, tm), BF16),
        ],
        compiler_params=pltpu.CompilerParams(
            dimension_semantics=("arbitrary", "arbitrary"),
            vmem_limit_bytes=VMEM_LIMIT_BYTES),
        name="mixer",
    )(x, x, g, wt, wrq, wk, wat, wbt, wout, sinks, gch, feat_tab, tok_tab, band, dmask, zeta, xi,
      w_up, w_down)


def _mlp_call(x2d, layer, g, wup, wdown, gfin, final_norm):
    n, d = x2d.shape
    tm = min(MLP_TILE, n)
    x_spec = pl.BlockSpec((tm, d), lambda i: (i, 0))
    return pl.pallas_call(
        functools.partial(_mlp_kernel, final_norm=final_norm),
        grid=(n // tm,),
        in_specs=[x_spec, _resident_layer(g, layer), _resident(wup.shape), _resident(wdown.shape),
                  _resident((1, d))],
        out_specs=x_spec,
        out_shape=jax.ShapeDtypeStruct(x2d.shape, F32),
        compiler_params=pltpu.CompilerParams(
            dimension_semantics=("arbitrary",),
            vmem_limit_bytes=VMEM_LIMIT_BYTES),
        name="mlp_final" if final_norm else "mlp",
    )(x2d, g, wup, wdown, gfin)


def _even_odd_heads(w):
    lead = w.shape[:-1]
    w = w.reshape(lead + (RET_HEADS, RET_QK_DIM // 2, 2))
    return jnp.swapaxes(w, -1, -2).reshape(lead + (RET_QK,))


def _halves_first(w):
    lead = w.shape[:-1]
    w = w.reshape(lead + (N_KV_HEADS, 2, HALF))
    return jnp.swapaxes(w, -2, -3).reshape(lead + (ATT_KV,))


def _prepare_in_projection(w_in):
    wt = jnp.swapaxes(w_in, 1, 2).astype(BF16)
    wrq = jnp.swapaxes(_even_odd_heads(w_in[:, :, OFF_RQ:OFF_RK]), 1, 2).astype(BF16)
    wk = jnp.concatenate([_halves_first(w_in[:, :, OFF_AK:OFF_AV]),
                          _even_odd_heads(w_in[:, :, OFF_RK:OFF_RV])], axis=2).astype(BF16)
    return wt, wrq, wk


def _as_f32(a):
    return np.ascontiguousarray(a, dtype=np.float32)


@functools.lru_cache(maxsize=None)
def _position_tables(seq):
    pos = np.arange(seq, dtype=np.float64)
    inv = ROPE_THETA ** (-np.arange(HALF, dtype=np.float64) / HALF)
    ang = pos[:, None] * inv[None, :]
    cos, sin = np.cos(ang), np.sin(ang)
    theta = 1.0 / (RET_THETA ** np.linspace(0.0, 1.0, RET_QK_DIM // 2))
    angr = pos[:, None] * theta[None, :]
    cosr, sinr = np.cos(angr), np.sin(angr)
    qs = HEAD_DIM ** -0.5 * LOG2E
    ks = RET_QK_DIM ** -0.5
    tok_tab = np.concatenate([cos, cos, cos, cos, -sin, -sin, sin, sin,
                              ks * cosr, ks * cosr, -ks * sinr, ks * sinr], axis=1)
    feat_tab = np.concatenate([qs * cos, qs * sin, cosr, sinr], axis=1).T

    kj = np.arange(2 * ATT_BLOCK)[:, None]
    qi = np.arange(GQA_GROUP * ATT_BLOCK)[None, :] % ATT_BLOCK
    inside = (kj > qi) & (kj <= qi + WINDOW)
    band = np.where(np.stack([inside, inside & (kj >= ATT_BLOCK)]), 0.0, -np.inf)

    log_g = np.log(1.0 - 2.0 ** (-5.0 - np.arange(RET_HEADS, dtype=np.float64)))
    idx = np.arange(RET_BLOCK, dtype=np.float64)
    rel = idx[None, :] - idx[:, None]
    dmask = np.where(rel[None] >= 0, np.exp(log_g[:, None, None] * np.maximum(rel, 0.0)[None]), 0.0)
    zeta = np.exp(log_g[:, None] * (RET_BLOCK - 1.0 - idx)[None, :])
    xi = np.exp(log_g[:, None] * (idx + 1.0)[None, :])
    zeta = np.broadcast_to(zeta[:, None, :], (RET_HEADS, 8, RET_BLOCK))
    xi = np.broadcast_to(xi[:, None, :], (RET_HEADS, 8, RET_BLOCK))
    gch = np.exp(log_g * RET_BLOCK)
    return tuple(_as_f32(a) for a in (feat_tab, tok_tab, band, dmask, zeta, xi)), _as_f32(gch)


def kernel(x, g_mix, w_in, sinks, w_a, w_b, w_out, g_mlp, w_up, w_down, g_final):
    bsz, seq, d = x.shape
    depth = w_in.shape[0]
    tables, gch = _position_tables(seq)
    wt, wrq, wk = _prepare_in_projection(w_in)
    wat = jnp.swapaxes(w_a, 1, 2).astype(BF16)
    wbt = jnp.swapaxes(w_b, 1, 2).astype(BF16)
    wout = w_out.astype(BF16)
    g_mix3 = g_mix[:, None, :]
    g_mlp3 = g_mlp[:, None, :]
    for l in range(depth):
        x, wup, wdown = _mixer_call(x, l, g_mix3, wt, wrq, wk, wat, wbt, wout, sinks[l], gch, tables,
                                    w_up, w_down)
        x = _mlp_call(x.reshape(bsz * seq, d), l, g_mlp3, wup, wdown, g_final[None, :],
                      final_norm=(l == depth - 1)).reshape(bsz, seq, d)
    return x
```

```python
import functools

import numpy as np
import jax
import jax.numpy as jnp
from jax import lax
from jax.experimental import pallas as pl
from jax.experimental.pallas import tpu as pltpu

D_MODEL = 1024
HEAD_DIM = 64
N_Q_HEADS = 8
N_KV_HEADS = 2
GQA_GROUP = N_Q_HEADS // N_KV_HEADS
WINDOW = 128
ROPE_THETA = 10000.0
RET_HEADS = 4
RET_QK_DIM = 128
RET_V_DIM = 2 * RET_QK_DIM
RET_THETA = 10000.0
D_FF = 4 * D_MODEL
EPS = 1e-6
LOG2E = 1.4426950408889634

ATT_Q = N_Q_HEADS * HEAD_DIM
ATT_KV = N_KV_HEADS * HEAD_DIM
RET_QK = RET_HEADS * RET_QK_DIM
RET_V = RET_HEADS * RET_V_DIM

OFF_AQ = 0
OFF_AK = OFF_AQ + ATT_Q
OFF_AV = OFF_AK + ATT_KV
OFF_RQ = OFF_AV + ATT_KV
OFF_RK = OFF_RQ + RET_QK
OFF_RV = OFF_RK + RET_QK
OFF_RG = OFF_RV + RET_V
OFF_GA = OFF_RG + RET_V
OFF_GB = OFF_GA + D_MODEL

LANES = 128
HALF = HEAD_DIM // 2

TOK_CKA, TOK_SKA, TOK_CKR, TOK_SKR = 0, LANES, 2 * LANES, 3 * LANES
FEAT_CQA = 0
FEAT_SQA = FEAT_CQA + HALF
FEAT_CQR = FEAT_SQA + HALF
FEAT_SQR = FEAT_CQR + RET_QK_DIM // 2
FEAT_END = FEAT_SQR + RET_QK_DIM // 2

ATT_BLOCK = WINDOW
RET_BLOCK = 256
SEQ_TILE = 512
ROW_BLOCK = 256
assert RET_V_DIM % ROW_BLOCK == 0 and ROW_BLOCK % RET_QK_DIM == 0
CAST_BLOCKS = 16
MLP_TILE = 1024
FF_CHUNK = 1024
VMEM_LIMIT_BYTES = 56 * 1024 * 1024

BF16 = jnp.bfloat16
F32 = jnp.float32


def _dot(a, b):
    return jnp.dot(a, b, preferred_element_type=F32)


def _dot_tn(a, b):
    return lax.dot_general(a, b, (((0,), (0,)), ((), ())), preferred_element_type=F32)


def _sigmoid(x):
    return 1.0 / (1.0 + jnp.exp2(x * (-LOG2E)))


def _rmsnorm(x, g):
    return x * lax.rsqrt(jnp.mean(x * x, axis=-1, keepdims=True) + EPS) * g


def _mixer_kernel(x_ref, xnext_ref, g_ref, wt_ref, wrq_ref, wk_ref, wat_ref, wbt_ref, wout_ref,
                  sinks_ref, gch_ref, feat_ref, tok_ref, band_ref, dmask_ref, zeta_ref, xi_ref,
                  wup_f32_ref, wdown_f32_ref,
                  o_ref, wup_bf16_ref, wdown_bf16_ref,
                  h_scr, ht_scr, qpad_scr, kband_scr, vt_scr, oatt_scr,
                  rq_scr, rqx_scr, rk_scr, rv_scr, rvz_scr, silu_scr, rb_scr, state_scr, mixed_scr, sgb_scr, mixedb_scr,
                  *, cast_every):
    tm = x_ref.shape[1]
    t = pl.program_id(1)

    @pl.when(t == 0)
    def _():
        kband_scr[0:ATT_BLOCK, :] = jnp.zeros((ATT_BLOCK, LANES), BF16)
        vt_scr[:, 0:ATT_BLOCK] = jnp.zeros((ATT_KV, ATT_BLOCK), BF16)
        qpad_scr[...] = jnp.zeros(qpad_scr.shape, BF16)
        state_scr[...] = jnp.zeros(state_scr.shape, F32)

    @pl.when((pl.program_id(0) * pl.num_programs(1) + t) % cast_every == 0)
    def _():
        wup_bf16_ref[...] = wup_f32_ref[...].astype(BF16)
        wdown_bf16_ref[...] = wdown_f32_ref[...].astype(BF16)

    def norm_to_scratch(xv):
        hb = _rmsnorm(xv, g_ref[...]).astype(BF16)
        h_scr[...] = hb
        ht_scr[...] = hb.T

    @pl.when((pl.program_id(0) == 0) & (t == 0))
    def _():
        norm_to_scratch(x_ref[0])

    zk = _dot(h_scr[...], wk_ref[...])

    def proj(r0, nrows):
        return _dot(wt_ref[r0:r0 + nrows, :], ht_scr[...])

    hq = RET_QK_DIM // 2

    def item_rk(hd):
        kk = zk[:, ATT_KV + RET_QK_DIM * hd:ATT_KV + RET_QK_DIM * (hd + 1)]
        rk_scr[:, RET_QK_DIM * hd:RET_QK_DIM * (hd + 1)] = (
            kk * tok_ref[:, TOK_CKR:TOK_CKR + LANES]
            + pltpu.roll(kk, LANES // 2, 1) * tok_ref[:, TOK_SKR:TOK_SKR + LANES]).astype(BF16)

    def item_rq(i):
        blk = _dot(wrq_ref[i * ROW_BLOCK:(i + 1) * ROW_BLOCK, :], ht_scr[...])
        cqr = feat_ref[FEAT_CQR:FEAT_SQR, :]
        sqr = feat_ref[FEAT_SQR:FEAT_END, :]
        heads_per_item = ROW_BLOCK // RET_QK_DIM
        for hl in range(heads_per_item):
            base = i * ROW_BLOCK + hl * RET_QK_DIM
            ev = blk[hl * RET_QK_DIM:hl * RET_QK_DIM + hq]
            od = blk[hl * RET_QK_DIM + hq:(hl + 1) * RET_QK_DIM]
            q_ev = ev * cqr - od * sqr
            q_od = od * cqr + ev * sqr
            rq_scr[base:base + hq, :] = q_ev.astype(BF16)
            rq_scr[base + hq:base + RET_QK_DIM, :] = q_od.astype(BF16)
            xi = jnp.concatenate([xi_ref[i * heads_per_item + hl][0:1, :]] * (tm // RET_BLOCK), axis=1)
            rqx_scr[base:base + hq, :] = (q_ev * xi).astype(BF16)
            rqx_scr[base + hq:base + RET_QK_DIM, :] = (q_od * xi).astype(BF16)

    def item_rv(i):
        rows = slice(i * ROW_BLOCK, (i + 1) * ROW_BLOCK)
        vt = proj(OFF_RV + i * ROW_BLOCK, ROW_BLOCK)
        zeta = jnp.concatenate([zeta_ref[i * ROW_BLOCK // RET_V_DIM][0:1, :]] * (tm // RET_BLOCK), axis=1)
        rv_scr[rows, :] = vt.astype(BF16)
        rvz_scr[rows, :] = (vt * zeta).astype(BF16)

    def item_rg(i):
        rg = proj(OFF_RG + i * ROW_BLOCK, ROW_BLOCK)
        silu_scr[i * ROW_BLOCK:(i + 1) * ROW_BLOCK, :] = rg * _sigmoid(rg)

    def item_ga(i):
        ga = proj(OFF_GA + i * ROW_BLOCK, ROW_BLOCK)
        ya = _dot(wat_ref[i * ROW_BLOCK:(i + 1) * ROW_BLOCK, :], oatt_scr[...])
        mixed_scr[i * ROW_BLOCK:(i + 1) * ROW_BLOCK, :] = _sigmoid(ga) * ya

    def item_gb(i):
        sgb_scr[i * ROW_BLOCK:(i + 1) * ROW_BLOCK, :] = _sigmoid(proj(OFF_GB + i * ROW_BLOCK, ROW_BLOCK))

    def spread(items, n_units):
        cuts = [round(u * len(items) / n_units) for u in range(n_units + 1)]
        return [items[cuts[u]:cuts[u + 1]] for u in range(n_units)]

    def item_aq(j):
        qt = proj(OFF_AQ + j * GQA_GROUP * HEAD_DIM, GQA_GROUP * HEAD_DIM)
        cq = feat_ref[FEAT_CQA:FEAT_SQA, :]
        sq = feat_ref[FEAT_SQA:FEAT_CQR, :]
        for g in range(GQA_GROUP):
            x1 = qt[HEAD_DIM * g:HEAD_DIM * g + HALF]
            x2 = qt[HEAD_DIM * g + HALF:HEAD_DIM * (g + 1)]
            qpad_scr[j, HALF * j:HALF * (j + 1), g * tm:(g + 1) * tm] = (
                x1 * cq - x2 * sq).astype(BF16)
            qpad_scr[j, 2 * HALF + HALF * j:2 * HALF + HALF * (j + 1), g * tm:(g + 1) * tm] = (
                x2 * cq + x1 * sq).astype(BF16)

    nblk = D_MODEL // ROW_BLOCK
    att_items = ([functools.partial(item_aq, j) for j in range(1, N_KV_HEADS)]
                 + [functools.partial(item_rk, hd) for hd in range(RET_HEADS)]
                 + [functools.partial(item_rq, i) for i in range(RET_QK // ROW_BLOCK)]
                 + [functools.partial(item_rv, i) for i in range(RET_V // ROW_BLOCK)]
                 + [functools.partial(item_rg, i) for i in range(RET_V // ROW_BLOCK)])
    ret_items = ([functools.partial(item_ga, i) for i in range(nblk)]
                 + [functools.partial(item_gb, i) for i in range(nblk)])

    vt_scr[:, ATT_BLOCK:ATT_BLOCK + tm] = proj(OFF_AV, ATT_KV).astype(BF16)
    ka = zk[:, 0:ATT_KV]
    kband_scr[ATT_BLOCK:ATT_BLOCK + tm, :] = (
        ka * tok_ref[:, TOK_CKA:TOK_CKA + LANES]
        + pltpu.roll(ka, LANES // 2, 1) * tok_ref[:, TOK_SKA:TOK_SKA + LANES]).astype(BF16)
    item_aq(0)

    band = band_ref[0]
    band_first = jnp.where(t > 0, band, band_ref[1])
    sink_rows = [
        jnp.concatenate([jnp.full((1, ATT_BLOCK), sinks_ref[GQA_GROUP * j + g] * LOG2E, F32)
                         for g in range(GQA_GROUP)], axis=1) for j in range(N_KV_HEADS)]

    def att_scores(c, j):
        r0 = c * ATT_BLOCK
        qs = jnp.concatenate(
            [qpad_scr[j, :, g * tm + r0:g * tm + r0 + ATT_BLOCK] for g in range(GQA_GROUP)],
            axis=1)
        kb = kband_scr[r0:r0 + 2 * ATT_BLOCK, :]
        return _dot(kb, qs) + (band_first if c == 0 else band)

    def att_finish(c, j, st):
        r0 = c * ATT_BLOCK
        m = jnp.maximum(jnp.max(st, axis=0, keepdims=True), sink_rows[j])
        p = jnp.exp2(st - m)
        denom = jnp.sum(p, axis=0, keepdims=True) + jnp.exp2(sink_rows[j] - m)
        vb = vt_scr[HEAD_DIM * j:HEAD_DIM * (j + 1), r0:r0 + 2 * ATT_BLOCK]
        ot = _dot(vb, p.astype(BF16)) * (1.0 / denom)
        for g in range(GQA_GROUP):
            hh = GQA_GROUP * j + g
            oatt_scr[HEAD_DIM * hh:HEAD_DIM * (hh + 1), r0:r0 + ATT_BLOCK] = (
                ot[:, g * ATT_BLOCK:(g + 1) * ATT_BLOCK].astype(BF16))

    att_units = [(c, j) for j in range(N_KV_HEADS) for c in range(tm // ATT_BLOCK)]
    att_sched = spread(att_items, len(att_units))
    st_next = att_scores(*att_units[0])
    for u, unit in enumerate(att_units):
        st_cur = st_next
        if u + 1 < len(att_units):
            st_next = att_scores(*att_units[u + 1])
        for item in att_sched[u]:
            item()
        att_finish(*unit, st_cur)

    kband_scr[0:ATT_BLOCK, :] = kband_scr[tm:tm + ATT_BLOCK, :]
    vt_scr[:, 0:ATT_BLOCK] = vt_scr[:, tm:tm + ATT_BLOCK]

    def ret_scores(c, hd):
        r0 = c * RET_BLOCK
        q = rq_scr[RET_QK_DIM * hd:RET_QK_DIM * (hd + 1), r0:r0 + RET_BLOCK]
        kk = rk_scr[r0:r0 + RET_BLOCK, RET_QK_DIM * hd:RET_QK_DIM * (hd + 1)]
        return (_dot(kk, q) * dmask_ref[hd]).astype(BF16)

    def ret_finish(c, hd, s):
        r0 = c * RET_BLOCK
        qx = rqx_scr[RET_QK_DIM * hd:RET_QK_DIM * (hd + 1), r0:r0 + RET_BLOCK]
        kk = rk_scr[r0:r0 + RET_BLOCK, RET_QK_DIM * hd:RET_QK_DIM * (hd + 1)]
        vv = rv_scr[RET_V_DIM * hd:RET_V_DIM * (hd + 1), r0:r0 + RET_BLOCK]
        vz = rvz_scr[RET_V_DIM * hd:RET_V_DIM * (hd + 1), r0:r0 + RET_BLOCK]
        state = state_scr[hd]
        out = _dot(jnp.concatenate([vv, state.astype(BF16)], axis=1),
                   jnp.concatenate([s, qx], axis=0))
        kv = _dot(vz, kk)
        state_scr[hd] = state * gch_ref[hd] + kv
        rn = out * lax.rsqrt(jnp.mean(out * out, axis=0, keepdims=True) + EPS)
        rb_scr[RET_V_DIM * hd:RET_V_DIM * (hd + 1), r0:r0 + RET_BLOCK] = (
            rn * silu_scr[RET_V_DIM * hd:RET_V_DIM * (hd + 1), r0:r0 + RET_BLOCK]).astype(BF16)

    ret_units = [(c, hd) for c in range(tm // RET_BLOCK) for hd in range(RET_HEADS)]
    ret_sched = spread(ret_items, len(ret_units))
    s_next = ret_scores(*ret_units[0])
    for u, unit in enumerate(ret_units):
        s_cur = s_next
        if u + 1 < len(ret_units):
            s_next = ret_scores(*ret_units[u + 1])
        for item in ret_sched[u]:
            item()
        ret_finish(*unit, s_cur)

    norm_to_scratch(xnext_ref[0])
    rb = rb_scr[...]
    for i in range(nblk):
        rows = slice(i * ROW_BLOCK, (i + 1) * ROW_BLOCK)
        yb = _dot(wbt_ref[rows, :], rb)
        mixedb_scr[rows, :] = (mixed_scr[rows, :] + sgb_scr[rows, :] * yb).astype(BF16)
    o_ref[0] = x_ref[0] + _dot_tn(mixedb_scr[...], wout_ref[...])


def _mlp_kernel(x_ref, g_ref, wup_ref, wdown_ref, gfin_ref, o_ref, *, final_norm):
    x = x_ref[...]
    h = _rmsnorm(x, g_ref[...]).astype(BF16)
    acc = jnp.zeros(x.shape, F32)
    for j in range(D_FF // FF_CHUNK):
        u = _dot(h, wup_ref[:, j * FF_CHUNK:(j + 1) * FF_CHUNK])
        a = jnp.square(jnp.maximum(u, 0.0)).astype(BF16)
        acc = acc + _dot(a, wdown_ref[j * FF_CHUNK:(j + 1) * FF_CHUNK, :])
    y = x + acc
    if final_norm:
        y = _rmsnorm(y, gfin_ref[...])
    o_ref[...] = y


def _resident(shape):
    return pl.BlockSpec(shape, lambda *_: (0,) * len(shape), pipeline_mode=pl.Buffered(1))


def _resident_layer(stacked, layer):
    tail = stacked.shape[1:]
    return pl.BlockSpec((None,) + tail, lambda *_: (layer,) + (0,) * len(tail),
                        pipeline_mode=pl.Buffered(1))


def _mixer_call(x, layer, g, wt, wrq, wk, wat, wbt, wout, sinks, gch, tables, w_up, w_down):
    bsz, seq, d = x.shape
    tm = min(SEQ_TILE, seq)
    feat_tab, tok_tab, band, dmask, zeta, xi = tables
    tok_spec = pl.BlockSpec((tm, tok_tab.shape[1]), lambda b, t: (t, 0))
    feat_spec = pl.BlockSpec((feat_tab.shape[0], tm), lambda b, t: (0, t))
    smem = pl.BlockSpec(memory_space=pltpu.SMEM)
    x_spec = pl.BlockSpec((1, tm, d), lambda b, t: (b, t, 0))
    nt = seq // tm

    def next_tile(b, t):
        flat = jnp.minimum(b * nt + t + 1, bsz * nt - 1)
        return (flat // nt, flat % nt, 0)

    steps = bsz * nt
    n_cast = max(nb for nb in (1, 2, 4, 8, 16) if steps % nb == 0 and nb <= CAST_BLOCKS)
    cast_every = steps // n_cast

    def cast_specs(w):
        rows, cols = w.shape[1] // n_cast, w.shape[2]
        blk = lambda b, t: ((b * nt + t) // cast_every, 0)
        return (pl.BlockSpec((None, rows, cols), lambda b, t: (layer,) + blk(b, t)),
                pl.BlockSpec((rows, cols), blk))

    (wup_in, wup_out), (wdown_in, wdown_out) = cast_specs(w_up), cast_specs(w_down)
    return pl.pallas_call(
        functools.partial(_mixer_kernel, cast_every=cast_every),
        grid=(bsz, nt),
        in_specs=[x_spec, pl.BlockSpec((1, tm, d), next_tile)]
        + [_resident_layer(w, layer) for w in (g, wt, wrq, wk, wat, wbt, wout)]
        + [smem, smem, feat_spec, tok_spec,
           _resident(band.shape), _resident(dmask.shape), _resident(zeta.shape), _resident(xi.shape),
           wup_in, wdown_in],
        out_specs=(x_spec, wup_out, wdown_out),
        out_shape=(jax.ShapeDtypeStruct(x.shape, F32),
                   jax.ShapeDtypeStruct(w_up.shape[1:], BF16),
                   jax.ShapeDtypeStruct(w_down.shape[1:], BF16)),
        scratch_shapes=[
            pltpu.VMEM((tm, d), BF16),
            pltpu.VMEM((d, tm), BF16),
            pltpu.VMEM((N_KV_HEADS, LANES, GQA_GROUP * tm), BF16),
            pltpu.VMEM((tm + ATT_BLOCK, LANES), BF16),
            pltpu.VMEM((ATT_KV, tm + ATT_BLOCK), BF16),
            pltpu.VMEM((ATT_Q, tm), BF16),
            pltpu.VMEM((RET_QK, tm), BF16),
            pltpu.VMEM((RET_QK, tm), BF16),
            pltpu.VMEM((tm, RET_QK), BF16),
            pltpu.VMEM((RET_V, tm), BF16),
            pltpu.VMEM((RET_V, tm), BF16),
            pltpu.VMEM((RET_V, tm), F32),
            pltpu.VMEM((RET_V, tm), BF16),
            pltpu.VMEM((RET_HEADS, RET_V_DIM, RET_QK_DIM), F32),
            pltpu.VMEM((d, tm), F32),
            pltpu.VMEM((d, tm), F32),
            pltpu.VMEM((d, tm), BF16),
        ],
        compiler_params=pltpu.CompilerParams(
            dimension_semantics=("arbitrary", "arbitrary"),
            vmem_limit_bytes=VMEM_LIMIT_BYTES),
        name="mixer",
    )(x, x, g, wt, wrq, wk, wat, wbt, wout, sinks, gch, feat_tab, tok_tab, band, dmask, zeta, xi,
      w_up, w_down)


def _mlp_call(x2d, layer, g, wup, wdown, gfin, final_norm):
    n, d = x2d.shape
    tm = min(MLP_TILE, n)
    x_spec = pl.BlockSpec((tm, d), lambda i: (i, 0))
    return pl.pallas_call(
        functools.partial(_mlp_kernel, final_norm=final_norm),
        grid=(n // tm,),
        in_specs=[x_spec, _resident_layer(g, layer), _resident(wup.shape), _resident(wdown.shape),
                  _resident((1, d))],
        out_specs=x_spec,
        out_shape=jax.ShapeDtypeStruct(x2d.shape, F32),
        compiler_params=pltpu.CompilerParams(
            dimension_semantics=("arbitrary",),
            vmem_limit_bytes=VMEM_LIMIT_BYTES),
        name="mlp_final" if final_norm else "mlp",
    )(x2d, g, wup, wdown, gfin)


def _even_odd_heads(w):
    lead = w.shape[:-1]
    w = w.reshape(lead + (RET_HEADS, RET_QK_DIM // 2, 2))
    return jnp.swapaxes(w, -1, -2).reshape(lead + (RET_QK,))


def _halves_first(w):
    lead = w.shape[:-1]
    w = w.reshape(lead + (N_KV_HEADS, 2, HALF))
    return jnp.swapaxes(w, -2, -3).reshape(lead + (ATT_KV,))


def _prepare_in_projection(w_in):
    wt = jnp.swapaxes(w_in, 1, 2).astype(BF16)
    wrq = jnp.swapaxes(_even_odd_heads(w_in[:, :, OFF_RQ:OFF_RK]), 1, 2).astype(BF16)
    wk = jnp.concatenate([_halves_first(w_in[:, :, OFF_AK:OFF_AV]),
                          _even_odd_heads(w_in[:, :, OFF_RK:OFF_RV])], axis=2).astype(BF16)
    return wt, wrq, wk


def _as_f32(a):
    return np.ascontiguousarray(a, dtype=np.float32)


@functools.lru_cache(maxsize=None)
def _position_tables(seq):
    pos = np.arange(seq, dtype=np.float64)
    inv = ROPE_THETA ** (-np.arange(HALF, dtype=np.float64) / HALF)
    ang = pos[:, None] * inv[None, :]
    cos, sin = np.cos(ang), np.sin(ang)
    theta = 1.0 / (RET_THETA ** np.linspace(0.0, 1.0, RET_QK_DIM // 2))
    angr = pos[:, None] * theta[None, :]
    cosr, sinr = np.cos(angr), np.sin(angr)
    qs = HEAD_DIM ** -0.5 * LOG2E
    ks = RET_QK_DIM ** -0.5
    tok_tab = np.concatenate([cos, cos, cos, cos, -sin, -sin, sin, sin,
                              ks * cosr, ks * cosr, -ks * sinr, ks * sinr], axis=1)
    feat_tab = np.concatenate([qs * cos, qs * sin, cosr, sinr], axis=1).T

    kj = np.arange(2 * ATT_BLOCK)[:, None]
    qi = np.arange(GQA_GROUP * ATT_BLOCK)[None, :] % ATT_BLOCK
    inside = (kj > qi) & (kj <= qi + WINDOW)
    band = np.where(np.stack([inside, inside & (kj >= ATT_BLOCK)]), 0.0, -np.inf)

    log_g = np.log(1.0 - 2.0 ** (-5.0 - np.arange(RET_HEADS, dtype=np.float64)))
    idx = np.arange(RET_BLOCK, dtype=np.float64)
    rel = idx[None, :] - idx[:, None]
    dmask = np.where(rel[None] >= 0, np.exp(log_g[:, None, None] * np.maximum(rel, 0.0)[None]), 0.0)
    zeta = np.exp(log_g[:, None] * (RET_BLOCK - 1.0 - idx)[None, :])
    xi = np.exp(log_g[:, None] * (idx + 1.0)[None, :])
    zeta = np.broadcast_to(zeta[:, None, :], (RET_HEADS, 8, RET_BLOCK))
    xi = np.broadcast_to(xi[:, None, :], (RET_HEADS, 8, RET_BLOCK))
    gch = np.exp(log_g * RET_BLOCK)
    return tuple(_as_f32(a) for a in (feat_tab, tok_tab, band, dmask, zeta, xi)), _as_f32(gch)


def kernel(x, g_mix, w_in, sinks, w_a, w_b, w_out, g_mlp, w_up, w_down, g_final):
    bsz, seq, d = x.shape
    depth = w_in.shape[0]
    tables, gch = _position_tables(seq)
    wt, wrq, wk = _prepare_in_projection(w_in)
    wat = jnp.swapaxes(w_a, 1, 2).astype(BF16)
    wbt = jnp.swapaxes(w_b, 1, 2).astype(BF16)
    wout = w_out.astype(BF16)
    g_mix3 = g_mix[:, None, :]
    g_mlp3 = g_mlp[:, None, :]
    for l in range(depth):
        x, wup, wdown = _mixer_call(x, l, g_mix3, wt, wrq, wk, wat, wbt, wout, sinks[l], gch, tables,
                                    w_up, w_down)
        x = _mlp_call(x.reshape(bsz * seq, d), l, g_mlp3, wup, wdown, g_final[None, :],
                      final_norm=(l == depth - 1)).reshape(bsz, seq, d)
    return x
```

```python
import functools

import numpy as np
import jax
import jax.numpy as jnp
from jax import lax
from jax.experimental import pallas as pl
from jax.experimental.pallas import tpu as pltpu

D_MODEL = 1024
HEAD_DIM = 64
N_Q_HEADS = 8
N_KV_HEADS = 2
GQA_GROUP = N_Q_HEADS // N_KV_HEADS
WINDOW = 128
ROPE_THETA = 10000.0
RET_HEADS = 4
RET_QK_DIM = 128
RET_V_DIM = 2 * RET_QK_DIM
RET_THETA = 10000.0
D_FF = 4 * D_MODEL
EPS = 1e-6
LOG2E = 1.4426950408889634

ATT_Q = N_Q_HEADS * HEAD_DIM
ATT_KV = N_KV_HEADS * HEAD_DIM
RET_QK = RET_HEADS * RET_QK_DIM
RET_V = RET_HEADS * RET_V_DIM

OFF_AQ = 0
OFF_AK = OFF_AQ + ATT_Q
OFF_AV = OFF_AK + ATT_KV
OFF_RQ = OFF_AV + ATT_KV
OFF_RK = OFF_RQ + RET_QK
OFF_RV = OFF_RK + RET_QK
OFF_RG = OFF_RV + RET_V
OFF_GA = OFF_RG + RET_V
OFF_GB = OFF_GA + D_MODEL

LANES = 128
HALF = HEAD_DIM // 2

TOK_CKA, TOK_SKA, TOK_CKR, TOK_SKR = 0, LANES, 2 * LANES, 3 * LANES
FEAT_CQA = 0
FEAT_SQA = FEAT_CQA + HALF
FEAT_CQR = FEAT_SQA + HALF
FEAT_SQR = FEAT_CQR + RET_QK_DIM // 2
FEAT_END = FEAT_SQR + RET_QK_DIM // 2

ATT_BLOCK = WINDOW
RET_BLOCK = 256
SEQ_TILE = 512
ROW_BLOCK = 256
assert RET_V_DIM % ROW_BLOCK == 0 and ROW_BLOCK % RET_QK_DIM == 0
CAST_BLOCKS = 16
MLP_TILE = 1024
MLP_INPUT_BUFFERS = 3
FF_CHUNK = 1024
VMEM_LIMIT_BYTES = 56 * 1024 * 1024

BF16 = jnp.bfloat16
F32 = jnp.float32


def _dot(a, b):
    return jnp.dot(a, b, preferred_element_type=F32)


def _dot_tn(a, b):
    return lax.dot_general(a, b, (((0,), (0,)), ((), ())), preferred_element_type=F32)


def _sigmoid(x):
    return 1.0 / (1.0 + jnp.exp2(x * (-LOG2E)))


def _rmsnorm(x, g):
    return x * lax.rsqrt(jnp.mean(x * x, axis=-1, keepdims=True) + EPS) * g


def _mixer_kernel(x_ref, xnext_ref, g_ref, wt_ref, wrq_ref, wk_ref, wat_ref, wbt_ref, wout_ref,
                  sinks_ref, gch_ref, feat_ref, tok_ref, band_ref, dmask_ref, zeta_ref, xi_ref,
                  wup_f32_ref, wdown_f32_ref,
                  o_ref, wup_bf16_ref, wdown_bf16_ref,
                  h_scr, ht_scr, qpad_scr, kband_scr, vt_scr, oatt_scr,
                  rq_scr, rqx_scr, rk_scr, rv_scr, rvz_scr, silu_scr, rb_scr, state_scr, mixed_scr, sgb_scr, mixedb_scr,
                  *, cast_every):
    tm = x_ref.shape[1]
    t = pl.program_id(1)

    @pl.when(t == 0)
    def _():
        kband_scr[0:ATT_BLOCK, :] = jnp.zeros((ATT_BLOCK, LANES), BF16)
        vt_scr[:, 0:ATT_BLOCK] = jnp.zeros((ATT_KV, ATT_BLOCK), BF16)
        qpad_scr[...] = jnp.zeros(qpad_scr.shape, BF16)
        state_scr[...] = jnp.zeros(state_scr.shape, F32)

    @pl.when((pl.program_id(0) * pl.num_programs(1) + t) % cast_every == 0)
    def _():
        wup_bf16_ref[...] = wup_f32_ref[...].astype(BF16)
        wdown_bf16_ref[...] = wdown_f32_ref[...].astype(BF16)

    def norm_to_scratch(xv):
        hb = _rmsnorm(xv, g_ref[...]).astype(BF16)
        h_scr[...] = hb
        ht_scr[...] = hb.T

    @pl.when((pl.program_id(0) == 0) & (t == 0))
    def _():
        norm_to_scratch(x_ref[0])

    zk = _dot(h_scr[...], wk_ref[...])

    def proj(r0, nrows):
        return _dot(wt_ref[r0:r0 + nrows, :], ht_scr[...])

    hq = RET_QK_DIM // 2

    def item_rk(hd):
        kk = zk[:, ATT_KV + RET_QK_DIM * hd:ATT_KV + RET_QK_DIM * (hd + 1)]
        rk_scr[:, RET_QK_DIM * hd:RET_QK_DIM * (hd + 1)] = (
            kk * tok_ref[:, TOK_CKR:TOK_CKR + LANES]
            + pltpu.roll(kk, LANES // 2, 1) * tok_ref[:, TOK_SKR:TOK_SKR + LANES]).astype(BF16)

    def item_rq(i):
        blk = _dot(wrq_ref[i * ROW_BLOCK:(i + 1) * ROW_BLOCK, :], ht_scr[...])
        cqr = feat_ref[FEAT_CQR:FEAT_SQR, :]
        sqr = feat_ref[FEAT_SQR:FEAT_END, :]
        heads_per_item = ROW_BLOCK // RET_QK_DIM
        for hl in range(heads_per_item):
            base = i * ROW_BLOCK + hl * RET_QK_DIM
            ev = blk[hl * RET_QK_DIM:hl * RET_QK_DIM + hq]
            od = blk[hl * RET_QK_DIM + hq:(hl + 1) * RET_QK_DIM]
            q_ev = ev * cqr - od * sqr
            q_od = od * cqr + ev * sqr
            rq_scr[base:base + hq, :] = q_ev.astype(BF16)
            rq_scr[base + hq:base + RET_QK_DIM, :] = q_od.astype(BF16)
            xi = jnp.concatenate([xi_ref[i * heads_per_item + hl][0:1, :]] * (tm // RET_BLOCK), axis=1)
            rqx_scr[base:base + hq, :] = (q_ev * xi).astype(BF16)
            rqx_scr[base + hq:base + RET_QK_DIM, :] = (q_od * xi).astype(BF16)

    def item_rv(i):
        rows = slice(i * ROW_BLOCK, (i + 1) * ROW_BLOCK)
        vt = proj(OFF_RV + i * ROW_BLOCK, ROW_BLOCK)
        zeta = jnp.concatenate([zeta_ref[i * ROW_BLOCK // RET_V_DIM][0:1, :]] * (tm // RET_BLOCK), axis=1)
        rv_scr[rows, :] = vt.astype(BF16)
        rvz_scr[rows, :] = (vt * zeta).astype(BF16)

    def item_rg(i):
        rg = proj(OFF_RG + i * ROW_BLOCK, ROW_BLOCK)
        silu_scr[i * ROW_BLOCK:(i + 1) * ROW_BLOCK, :] = rg * _sigmoid(rg)

    def item_ga(i):
        ga = proj(OFF_GA + i * ROW_BLOCK, ROW_BLOCK)
        ya = _dot(wat_ref[i * ROW_BLOCK:(i + 1) * ROW_BLOCK, :], oatt_scr[...])
        mixed_scr[i * ROW_BLOCK:(i + 1) * ROW_BLOCK, :] = _sigmoid(ga) * ya

    def item_gb(i):
        sgb_scr[i * ROW_BLOCK:(i + 1) * ROW_BLOCK, :] = _sigmoid(proj(OFF_GB + i * ROW_BLOCK, ROW_BLOCK))

    def spread(items, n_units):
        cuts = [round(u * len(items) / n_units) for u in range(n_units + 1)]
        return [items[cuts[u]:cuts[u + 1]] for u in range(n_units)]

    def item_aq(j):
        qt = proj(OFF_AQ + j * GQA_GROUP * HEAD_DIM, GQA_GROUP * HEAD_DIM)
        cq = feat_ref[FEAT_CQA:FEAT_SQA, :]
        sq = feat_ref[FEAT_SQA:FEAT_CQR, :]
        for g in range(GQA_GROUP):
            x1 = qt[HEAD_DIM * g:HEAD_DIM * g + HALF]
            x2 = qt[HEAD_DIM * g + HALF:HEAD_DIM * (g + 1)]
            qpad_scr[j, HALF * j:HALF * (j + 1), g * tm:(g + 1) * tm] = (
                x1 * cq - x2 * sq).astype(BF16)
            qpad_scr[j, 2 * HALF + HALF * j:2 * HALF + HALF * (j + 1), g * tm:(g + 1) * tm] = (
                x2 * cq + x1 * sq).astype(BF16)

    nblk = D_MODEL // ROW_BLOCK
    att_items = ([functools.partial(item_aq, j) for j in range(1, N_KV_HEADS)]
                 + [functools.partial(item_rk, hd) for hd in range(RET_HEADS)]
                 + [functools.partial(item_rq, i) for i in range(RET_QK // ROW_BLOCK)]
                 + [functools.partial(item_rv, i) for i in range(RET_V // ROW_BLOCK)]
                 + [functools.partial(item_rg, i) for i in range(RET_V // ROW_BLOCK)])
    ret_items = ([functools.partial(item_ga, i) for i in range(nblk)]
                 + [functools.partial(item_gb, i) for i in range(nblk)])

    vt_scr[:, ATT_BLOCK:ATT_BLOCK + tm] = proj(OFF_AV, ATT_KV).astype(BF16)
    ka = zk[:, 0:ATT_KV]
    kband_scr[ATT_BLOCK:ATT_BLOCK + tm, :] = (
        ka * tok_ref[:, TOK_CKA:TOK_CKA + LANES]
        + pltpu.roll(ka, LANES // 2, 1) * tok_ref[:, TOK_SKA:TOK_SKA + LANES]).astype(BF16)
    item_aq(0)

    band = band_ref[0]
    band_first = jnp.where(t > 0, band, band_ref[1])
    sink_rows = [
        jnp.concatenate([jnp.full((1, ATT_BLOCK), sinks_ref[GQA_GROUP * j + g] * LOG2E, F32)
                         for g in range(GQA_GROUP)], axis=1) for j in range(N_KV_HEADS)]

    def att_scores(c, j):
        r0 = c * ATT_BLOCK
        qs = jnp.concatenate(
            [qpad_scr[j, :, g * tm + r0:g * tm + r0 + ATT_BLOCK] for g in range(GQA_GROUP)],
            axis=1)
        kb = kband_scr[r0:r0 + 2 * ATT_BLOCK, :]
        return _dot(kb, qs) + (band_first if c == 0 else band)

    def att_finish(c, j, st):
        r0 = c * ATT_BLOCK
        m = jnp.maximum(jnp.max(st, axis=0, keepdims=True), sink_rows[j])
        p = jnp.exp2(st - m)
        denom = jnp.sum(p, axis=0, keepdims=True) + jnp.exp2(sink_rows[j] - m)
        vb = vt_scr[HEAD_DIM * j:HEAD_DIM * (j + 1), r0:r0 + 2 * ATT_BLOCK]
        ot = _dot(vb, p.astype(BF16)) * (1.0 / denom)
        for g in range(GQA_GROUP):
            hh = GQA_GROUP * j + g
            oatt_scr[HEAD_DIM * hh:HEAD_DIM * (hh + 1), r0:r0 + ATT_BLOCK] = (
                ot[:, g * ATT_BLOCK:(g + 1) * ATT_BLOCK].astype(BF16))

    att_units = [(c, j) for j in range(N_KV_HEADS) for c in range(tm // ATT_BLOCK)]
    att_sched = spread(att_items, len(att_units))
    st_next = att_scores(*att_units[0])
    for u, unit in enumerate(att_units):
        st_cur = st_next
        if u + 1 < len(att_units):
            st_next = att_scores(*att_units[u + 1])
        for item in att_sched[u]:
            item()
        att_finish(*unit, st_cur)

    kband_scr[0:ATT_BLOCK, :] = kband_scr[tm:tm + ATT_BLOCK, :]
    vt_scr[:, 0:ATT_BLOCK] = vt_scr[:, tm:tm + ATT_BLOCK]

    def ret_scores(c, hd):
        r0 = c * RET_BLOCK
        q = rq_scr[RET_QK_DIM * hd:RET_QK_DIM * (hd + 1), r0:r0 + RET_BLOCK]
        kk = rk_scr[r0:r0 + RET_BLOCK, RET_QK_DIM * hd:RET_QK_DIM * (hd + 1)]
        return (_dot(kk, q) * dmask_ref[hd]).astype(BF16)

    def ret_finish(c, hd, s):
        r0 = c * RET_BLOCK
        qx = rqx_scr[RET_QK_DIM * hd:RET_QK_DIM * (hd + 1), r0:r0 + RET_BLOCK]
        kk = rk_scr[r0:r0 + RET_BLOCK, RET_QK_DIM * hd:RET_QK_DIM * (hd + 1)]
        vv = rv_scr[RET_V_DIM * hd:RET_V_DIM * (hd + 1), r0:r0 + RET_BLOCK]
        vz = rvz_scr[RET_V_DIM * hd:RET_V_DIM * (hd + 1), r0:r0 + RET_BLOCK]
        state = state_scr[hd]
        out = _dot(jnp.concatenate([vv, state.astype(BF16)], axis=1),
                   jnp.concatenate([s, qx], axis=0))
        kv = _dot(vz, kk)
        state_scr[hd] = state * gch_ref[hd] + kv
        rn = out * lax.rsqrt(jnp.mean(out * out, axis=0, keepdims=True) + EPS)
        rb_scr[RET_V_DIM * hd:RET_V_DIM * (hd + 1), r0:r0 + RET_BLOCK] = (
            rn * silu_scr[RET_V_DIM * hd:RET_V_DIM * (hd + 1), r0:r0 + RET_BLOCK]).astype(BF16)

    ret_units = [(c, hd) for c in range(tm // RET_BLOCK) for hd in range(RET_HEADS)]
    ret_sched = spread(ret_items, len(ret_units))
    s_next = ret_scores(*ret_units[0])
    for u, unit in enumerate(ret_units):
        s_cur = s_next
        if u + 1 < len(ret_units):
            s_next = ret_scores(*ret_units[u + 1])
        for item in ret_sched[u]:
            item()
        ret_finish(*unit, s_cur)

    norm_to_scratch(xnext_ref[0])
    rb = rb_scr[...]
    for i in range(nblk):
        rows = slice(i * ROW_BLOCK, (i + 1) * ROW_BLOCK)
        yb = _dot(wbt_ref[rows, :], rb)
        mixedb_scr[rows, :] = (mixed_scr[rows, :] + sgb_scr[rows, :] * yb).astype(BF16)
    o_ref[0] = x_ref[0] + _dot_tn(mixedb_scr[...], wout_ref[...])


def _mlp_kernel(x_hbm, g_ref, wup_ref, wdown_ref, gfin_ref, o_ref, a_scr, x_buf, x_sem,
                *, final_norm, n_steps):
    i = pl.program_id(0)
    tm = o_ref.shape[0]
    ahead = MLP_INPUT_BUFFERS - 1

    def tile_copy(tile):
        slot = tile % MLP_INPUT_BUFFERS
        return pltpu.make_async_copy(x_hbm.at[pl.ds(tile * tm, tm), :], x_buf.at[slot], x_sem.at[slot])

    @pl.when(i == 0)
    def _():
        for k in range(min(ahead, n_steps)):
            tile_copy(k).start()

    @pl.when(i + ahead < n_steps)
    def _():
        tile_copy(i + ahead).start()

    tile_copy(i).wait()
    x = x_buf[i % MLP_INPUT_BUFFERS]
    h = _rmsnorm(x, g_ref[...]).astype(BF16)
    for j in range(D_FF // FF_CHUNK):
        u = _dot(h, wup_ref[:, j * FF_CHUNK:(j + 1) * FF_CHUNK])
        a_scr[:, j * FF_CHUNK:(j + 1) * FF_CHUNK] = jnp.square(jnp.maximum(u, 0.0)).astype(BF16)
    y = x + _dot(a_scr[...], wdown_ref[...])
    if final_norm:
        y = _rmsnorm(y, gfin_ref[...])
    o_ref[...] = y


def _resident(shape):
    return pl.BlockSpec(shape, lambda *_: (0,) * len(shape), pipeline_mode=pl.Buffered(1))


def _resident_layer(stacked, layer):
    tail = stacked.shape[1:]
    return pl.BlockSpec((None,) + tail, lambda *_: (layer,) + (0,) * len(tail),
                        pipeline_mode=pl.Buffered(1))


def _mixer_call(x, layer, g, wt, wrq, wk, wat, wbt, wout, sinks, gch, tables, w_up, w_down):
    bsz, seq, d = x.shape
    tm = min(SEQ_TILE, seq)
    feat_tab, tok_tab, band, dmask, zeta, xi = tables
    tok_spec = pl.BlockSpec((tm, tok_tab.shape[1]), lambda b, t: (t, 0))
    feat_spec = pl.BlockSpec((feat_tab.shape[0], tm), lambda b, t: (0, t))
    smem = pl.BlockSpec(memory_space=pltpu.SMEM)
    x_spec = pl.BlockSpec((1, tm, d), lambda b, t: (b, t, 0))
    nt = seq // tm

    def next_tile(b, t):
        flat = jnp.minimum(b * nt + t + 1, bsz * nt - 1)
        return (flat // nt, flat % nt, 0)

    steps = bsz * nt
    n_cast = max(nb for nb in (1, 2, 4, 8, 16) if steps % nb == 0 and nb <= CAST_BLOCKS)
    cast_every = steps // n_cast

    def cast_specs(w):
        rows, cols = w.shape[1] // n_cast, w.shape[2]
        blk = lambda b, t: ((b * nt + t) // cast_every, 0)
        return (pl.BlockSpec((None, rows, cols), lambda b, t: (layer,) + blk(b, t)),
                pl.BlockSpec((rows, cols), blk))

    (wup_in, wup_out), (wdown_in, wdown_out) = cast_specs(w_up), cast_specs(w_down)
    return pl.pallas_call(
        functools.partial(_mixer_kernel, cast_every=cast_every),
        grid=(bsz, nt),
        in_specs=[x_spec, pl.BlockSpec((1, tm, d), next_tile)]
        + [_resident_layer(w, layer) for w in (g, wt, wrq, wk, wat, wbt, wout)]
        + [smem, smem, feat_spec, tok_spec,
           _resident(band.shape), _resident(dmask.shape), _resident(zeta.shape), _resident(xi.shape),
           wup_in, wdown_in],
        out_specs=(x_spec, wup_out, wdown_out),
        out_shape=(jax.ShapeDtypeStruct(x.shape, F32),
                   jax.ShapeDtypeStruct(w_up.shape[1:], BF16),
                   jax.ShapeDtypeStruct(w_down.shape[1:], BF16)),
        scratch_shapes=[
            pltpu.VMEM((tm, d), BF16),
            pltpu.VMEM((d, tm), BF16),
            pltpu.VMEM((N_KV_HEADS, LANES, GQA_GROUP * tm), BF16),
            pltpu.VMEM((tm + ATT_BLOCK, LANES), BF16),
            pltpu.VMEM((ATT_KV, tm + ATT_BLOCK), BF16),
            pltpu.VMEM((ATT_Q, tm), BF16),
            pltpu.VMEM((RET_QK, tm), BF16),
            pltpu.VMEM((RET_QK, tm), BF16),
            pltpu.VMEM((tm, RET_QK), BF16),
            pltpu.VMEM((RET_V, tm), BF16),
            pltpu.VMEM((RET_V, tm), BF16),
            pltpu.VMEM((RET_V, tm), F32),
            pltpu.VMEM((RET_V, tm), BF16),
            pltpu.VMEM((RET_HEADS, RET_V_DIM, RET_QK_DIM), F32),
            pltpu.VMEM((d, tm), F32),
            pltpu.VMEM((d, tm), F32),
            pltpu.VMEM((d, tm), BF16),
        ],
        compiler_params=pltpu.CompilerParams(
            dimension_semantics=("arbitrary", "arbitrary"),
            vmem_limit_bytes=VMEM_LIMIT_BYTES),
        name="mixer",
    )(x, x, g, wt, wrq, wk, wat, wbt, wout, sinks, gch, feat_tab, tok_tab, band, dmask, zeta, xi,
      w_up, w_down)


def _mlp_call(x2d, layer, g, wup, wdown, gfin, final_norm):
    n, d = x2d.shape
    tm = min(MLP_TILE, n)
    x_spec = pl.BlockSpec((tm, d), lambda i: (i, 0))
    return pl.pallas_call(
        functools.partial(_mlp_kernel, final_norm=final_norm, n_steps=n // tm),
        grid=(n // tm,),
        in_specs=[pl.BlockSpec(memory_space=pl.ANY), _resident_layer(g, layer), _resident(wup.shape),
                  _resident(wdown.shape), _resident((1, d))],
        out_specs=x_spec,
        out_shape=jax.ShapeDtypeStruct(x2d.shape, F32),
        scratch_shapes=[pltpu.VMEM((tm, D_FF), BF16),
                        pltpu.VMEM((MLP_INPUT_BUFFERS, tm, d), F32),
                        pltpu.SemaphoreType.DMA((MLP_INPUT_BUFFERS,))],
        compiler_params=pltpu.CompilerParams(
            dimension_semantics=("arbitrary",),
            vmem_limit_bytes=VMEM_LIMIT_BYTES),
        name="mlp_final" if final_norm else "mlp",
    )(x2d, g, wup, wdown, gfin)


def _even_odd_heads(w):
    lead = w.shape[:-1]
    w = w.reshape(lead + (RET_HEADS, RET_QK_DIM // 2, 2))
    return jnp.swapaxes(w, -1, -2).reshape(lead + (RET_QK,))


def _halves_first(w):
    lead = w.shape[:-1]
    w = w.reshape(lead + (N_KV_HEADS, 2, HALF))
    return jnp.swapaxes(w, -2, -3).reshape(lead + (ATT_KV,))


def _prepare_in_projection(w_in):
    wt = jnp.swapaxes(w_in, 1, 2).astype(BF16)
    wrq = jnp.swapaxes(_even_odd_heads(w_in[:, :, OFF_RQ:OFF_RK]), 1, 2).astype(BF16)
    wk = jnp.concatenate([_halves_first(w_in[:, :, OFF_AK:OFF_AV]),
                          _even_odd_heads(w_in[:, :, OFF_RK:OFF_RV])], axis=2).astype(BF16)
    return wt, wrq, wk


def _as_f32(a):
    return np.ascontiguousarray(a, dtype=np.float32)


@functools.lru_cache(maxsize=None)
def _position_tables(seq):
    pos = np.arange(seq, dtype=np.float64)
    inv = ROPE_THETA ** (-np.arange(HALF, dtype=np.float64) / HALF)
    ang = pos[:, None] * inv[None, :]
    cos, sin = np.cos(ang), np.sin(ang)
    theta = 1.0 / (RET_THETA ** np.linspace(0.0, 1.0, RET_QK_DIM // 2))
    angr = pos[:, None] * theta[None, :]
    cosr, sinr = np.cos(angr), np.sin(angr)
    qs = HEAD_DIM ** -0.5 * LOG2E
    ks = RET_QK_DIM ** -0.5
    tok_tab = np.concatenate([cos, cos, cos, cos, -sin, -sin, sin, sin,
                              ks * cosr, ks * cosr, -ks * sinr, ks * sinr], axis=1)
    feat_tab = np.concatenate([qs * cos, qs * sin, cosr, sinr], axis=1).T

    kj = np.arange(2 * ATT_BLOCK)[:, None]
    qi = np.arange(GQA_GROUP * ATT_BLOCK)[None, :] % ATT_BLOCK
    inside = (kj > qi) & (kj <= qi + WINDOW)
    band = np.where(np.stack([inside, inside & (kj >= ATT_BLOCK)]), 0.0, -np.inf)

    log_g = np.log(1.0 - 2.0 ** (-5.0 - np.arange(RET_HEADS, dtype=np.float64)))
    idx = np.arange(RET_BLOCK, dtype=np.float64)
    rel = idx[None, :] - idx[:, None]
    dmask = np.where(rel[None] >= 0, np.exp(log_g[:, None, None] * np.maximum(rel, 0.0)[None]), 0.0)
    zeta = np.exp(log_g[:, None] * (RET_BLOCK - 1.0 - idx)[None, :])
    xi = np.exp(log_g[:, None] * (idx + 1.0)[None, :])
    zeta = np.broadcast_to(zeta[:, None, :], (RET_HEADS, 8, RET_BLOCK))
    xi = np.broadcast_to(xi[:, None, :], (RET_HEADS, 8, RET_BLOCK))
    gch = np.exp(log_g * RET_BLOCK)
    return tuple(_as_f32(a) for a in (feat_tab, tok_tab, band, dmask, zeta, xi)), _as_f32(gch)


def kernel(x, g_mix, w_in, sinks, w_a, w_b, w_out, g_mlp, w_up, w_down, g_final):
    bsz, seq, d = x.shape
    depth = w_in.shape[0]
    tables, gch = _position_tables(seq)
    wt, wrq, wk = _prepare_in_projection(w_in)
    wat = jnp.swapaxes(w_a, 1, 2).astype(BF16)
    wbt = jnp.swapaxes(w_b, 1, 2).astype(BF16)
    wout = w_out.astype(BF16)
    g_mix3 = g_mix[:, None, :]
    g_mlp3 = g_mlp[:, None, :]
    for l in range(depth):
        x, wup, wdown = _mixer_call(x, l, g_mix3, wt, wrq, wk, wat, wbt, wout, sinks[l], gch, tables,
                                    w_up, w_down)
        x = _mlp_call(x.reshape(bsz * seq, d), l, g_mlp3, wup, wdown, g_final[None, :],
                      final_norm=(l == depth - 1)).reshape(bsz, seq, d)
    return x
```

```python
import functools

import numpy as np
import jax
import jax.numpy as jnp
from jax import lax
from jax.experimental import pallas as pl
from jax.experimental.pallas import tpu as pltpu

D_MODEL = 1024
HEAD_DIM = 64
N_Q_HEADS = 8
N_KV_HEADS = 2
GQA_GROUP = N_Q_HEADS // N_KV_HEADS
WINDOW = 128
ROPE_THETA = 10000.0
RET_HEADS = 4
RET_QK_DIM = 128
RET_V_DIM = 2 * RET_QK_DIM
RET_THETA = 10000.0
D_FF = 4 * D_MODEL
EPS = 1e-6
LOG2E = 1.4426950408889634

ATT_Q = N_Q_HEADS * HEAD_DIM
ATT_KV = N_KV_HEADS * HEAD_DIM
RET_QK = RET_HEADS * RET_QK_DIM
RET_V = RET_HEADS * RET_V_DIM

OFF_AQ = 0
OFF_AK = OFF_AQ + ATT_Q
OFF_AV = OFF_AK + ATT_KV
OFF_RQ = OFF_AV + ATT_KV
OFF_RK = OFF_RQ + RET_QK
OFF_RV = OFF_RK + RET_QK
OFF_RG = OFF_RV + RET_V
OFF_GA = OFF_RG + RET_V
OFF_GB = OFF_GA + D_MODEL

LANES = 128
HALF = HEAD_DIM // 2

TOK_CKA, TOK_SKA, TOK_CKR, TOK_SKR = 0, LANES, 2 * LANES, 3 * LANES
FEAT_CQA = 0
FEAT_SQA = FEAT_CQA + HALF
FEAT_CQR = FEAT_SQA + HALF
FEAT_SQR = FEAT_CQR + RET_QK_DIM // 2
FEAT_END = FEAT_SQR + RET_QK_DIM // 2

ATT_BLOCK = WINDOW
RET_BLOCK = 256
SEQ_TILE = 512
ROW_BLOCK = 256
assert RET_V_DIM % ROW_BLOCK == 0 and ROW_BLOCK % RET_QK_DIM == 0
CAST_BLOCKS = 64
MLP_TILE = 1024
FF_CHUNK = 1024
VMEM_LIMIT_BYTES = 56 * 1024 * 1024

BF16 = jnp.bfloat16
F32 = jnp.float32


def _dot(a, b):
    return jnp.dot(a, b, preferred_element_type=F32)


def _dot_tn(a, b):
    return lax.dot_general(a, b, (((0,), (0,)), ((), ())), preferred_element_type=F32)


def _sigmoid(x):
    return 1.0 / (1.0 + jnp.exp2(x * (-LOG2E)))


def _rmsnorm(x, g):
    return x * lax.rsqrt(jnp.mean(x * x, axis=-1, keepdims=True) + EPS) * g


def _mixer_kernel(x_ref, xnext_ref, g_ref, wt_ref, wrq_ref, wk_ref, wat_ref, wbt_ref, wout_ref,
                  sinks_ref, gch_ref, feat_ref, tok_ref, band_ref, dmask_ref, zeta_ref, xi_ref,
                  wup_f32_ref, wdown_f32_ref,
                  o_ref, wup_bf16_ref, wdown_bf16_ref,
                  h_scr, ht_scr, qpad_scr, kband_scr, vt_scr, oatt_scr,
                  rq_scr, rqx_scr, rk_scr, rv_scr, rvz_scr, silu_scr, rb_scr, state_scr, mixed_scr, sgb_scr, mixedb_scr,
                  *, cast_every):
    tm = x_ref.shape[1]
    t = pl.program_id(1)

    @pl.when(t == 0)
    def _():
        kband_scr[0:ATT_BLOCK, :] = jnp.zeros((ATT_BLOCK, LANES), BF16)
        vt_scr[:, 0:ATT_BLOCK] = jnp.zeros((ATT_KV, ATT_BLOCK), BF16)
        qpad_scr[...] = jnp.zeros(qpad_scr.shape, BF16)
        state_scr[...] = jnp.zeros(state_scr.shape, F32)

    def cast_mlp_weights():
        wup_bf16_ref[...] = wup_f32_ref[...].astype(BF16)
        wdown_bf16_ref[...] = wdown_f32_ref[...].astype(BF16)

    if cast_every == 1:
        cast_mlp_weights()
    else:
        pl.when((pl.program_id(0) * pl.num_programs(1) + t) % cast_every == 0)(cast_mlp_weights)

    def norm_to_scratch(xv):
        hb = _rmsnorm(xv, g_ref[...]).astype(BF16)
        h_scr[...] = hb
        ht_scr[...] = hb.T

    @pl.when((pl.program_id(0) == 0) & (t == 0))
    def _():
        norm_to_scratch(x_ref[0])

    zk = _dot(h_scr[...], wk_ref[...])

    def proj(r0, nrows):
        return _dot(wt_ref[r0:r0 + nrows, :], ht_scr[...])

    hq = RET_QK_DIM // 2

    def item_rk(hd):
        kk = zk[:, ATT_KV + RET_QK_DIM * hd:ATT_KV + RET_QK_DIM * (hd + 1)]
        rk_scr[:, RET_QK_DIM * hd:RET_QK_DIM * (hd + 1)] = (
            kk * tok_ref[:, TOK_CKR:TOK_CKR + LANES]
            + pltpu.roll(kk, LANES // 2, 1) * tok_ref[:, TOK_SKR:TOK_SKR + LANES]).astype(BF16)

    def item_rq(i):
        blk = _dot(wrq_ref[i * ROW_BLOCK:(i + 1) * ROW_BLOCK, :], ht_scr[...])
        cqr = feat_ref[FEAT_CQR:FEAT_SQR, :]
        sqr = feat_ref[FEAT_SQR:FEAT_END, :]
        heads_per_item = ROW_BLOCK // RET_QK_DIM
        for hl in range(heads_per_item):
            base = i * ROW_BLOCK + hl * RET_QK_DIM
            ev = blk[hl * RET_QK_DIM:hl * RET_QK_DIM + hq]
            od = blk[hl * RET_QK_DIM + hq:(hl + 1) * RET_QK_DIM]
            q_ev = ev * cqr - od * sqr
            q_od = od * cqr + ev * sqr
            rq_scr[base:base + hq, :] = q_ev.astype(BF16)
            rq_scr[base + hq:base + RET_QK_DIM, :] = q_od.astype(BF16)
            xi = jnp.concatenate([xi_ref[i * heads_per_item + hl][0:1, :]] * (tm // RET_BLOCK), axis=1)
            rqx_scr[base:base + hq, :] = (q_ev * xi).astype(BF16)
            rqx_scr[base + hq:base + RET_QK_DIM, :] = (q_od * xi).astype(BF16)

    def item_rv(i):
        rows = slice(i * ROW_BLOCK, (i + 1) * ROW_BLOCK)
        vt = proj(OFF_RV + i * ROW_BLOCK, ROW_BLOCK)
        zeta = jnp.concatenate([zeta_ref[i * ROW_BLOCK // RET_V_DIM][0:1, :]] * (tm // RET_BLOCK), axis=1)
        rv_scr[rows, :] = vt.astype(BF16)
        rvz_scr[rows, :] = (vt * zeta).astype(BF16)

    def item_rg(i):
        rg = proj(OFF_RG + i * ROW_BLOCK, ROW_BLOCK)
        silu_scr[i * ROW_BLOCK:(i + 1) * ROW_BLOCK, :] = rg * _sigmoid(rg)

    def item_ga(i):
        ga = proj(OFF_GA + i * ROW_BLOCK, ROW_BLOCK)
        ya = _dot(wat_ref[i * ROW_BLOCK:(i + 1) * ROW_BLOCK, :], oatt_scr[...])
        mixed_scr[i * ROW_BLOCK:(i + 1) * ROW_BLOCK, :] = _sigmoid(ga) * ya

    def item_gb(i):
        sgb_scr[i * ROW_BLOCK:(i + 1) * ROW_BLOCK, :] = _sigmoid(proj(OFF_GB + i * ROW_BLOCK, ROW_BLOCK))

    def spread(items, n_units):
        cuts = [round(u * len(items) / n_units) for u in range(n_units + 1)]
        return [items[cuts[u]:cuts[u + 1]] for u in range(n_units)]

    def item_aq(j):
        qt = proj(OFF_AQ + j * GQA_GROUP * HEAD_DIM, GQA_GROUP * HEAD_DIM)
        cq = feat_ref[FEAT_CQA:FEAT_SQA, :]
        sq = feat_ref[FEAT_SQA:FEAT_CQR, :]
        for g in range(GQA_GROUP):
            x1 = qt[HEAD_DIM * g:HEAD_DIM * g + HALF]
            x2 = qt[HEAD_DIM * g + HALF:HEAD_DIM * (g + 1)]
            qpad_scr[j, HALF * j:HALF * (j + 1), g * tm:(g + 1) * tm] = (
                x1 * cq - x2 * sq).astype(BF16)
            qpad_scr[j, 2 * HALF + HALF * j:2 * HALF + HALF * (j + 1), g * tm:(g + 1) * tm] = (
                x2 * cq + x1 * sq).astype(BF16)

    nblk = D_MODEL // ROW_BLOCK
    att_items = ([functools.partial(item_aq, j) for j in range(1, N_KV_HEADS)]
                 + [functools.partial(item_rk, hd) for hd in range(RET_HEADS)]
                 + [functools.partial(item_rq, i) for i in range(RET_QK // ROW_BLOCK)]
                 + [functools.partial(item_rv, i) for i in range(RET_V // ROW_BLOCK)]
                 + [functools.partial(item_rg, i) for i in range(RET_V // ROW_BLOCK)])
    ret_items = ([functools.partial(item_ga, i) for i in range(nblk)]
                 + [functools.partial(item_gb, i) for i in range(nblk)])

    vt_scr[:, ATT_BLOCK:ATT_BLOCK + tm] = proj(OFF_AV, ATT_KV).astype(BF16)
    ka = zk[:, 0:ATT_KV]
    kband_scr[ATT_BLOCK:ATT_BLOCK + tm, :] = (
        ka * tok_ref[:, TOK_CKA:TOK_CKA + LANES]
        + pltpu.roll(ka, LANES // 2, 1) * tok_ref[:, TOK_SKA:TOK_SKA + LANES]).astype(BF16)
    item_aq(0)

    band = band_ref[0]
    band_first = jnp.where(t > 0, band, band_ref[1])
    sink_rows = [
        jnp.concatenate([jnp.full((1, ATT_BLOCK), sinks_ref[GQA_GROUP * j + g] * LOG2E, F32)
                         for g in range(GQA_GROUP)], axis=1) for j in range(N_KV_HEADS)]

    def att_scores(c, j):
        r0 = c * ATT_BLOCK
        qs = jnp.concatenate(
            [qpad_scr[j, :, g * tm + r0:g * tm + r0 + ATT_BLOCK] for g in range(GQA_GROUP)],
            axis=1)
        kb = kband_scr[r0:r0 + 2 * ATT_BLOCK, :]
        return _dot(kb, qs) + (band_first if c == 0 else band)

    def att_finish(c, j, st):
        r0 = c * ATT_BLOCK
        m = jnp.maximum(jnp.max(st, axis=0, keepdims=True), sink_rows[j])
        p = jnp.exp2(st - m)
        denom = jnp.sum(p, axis=0, keepdims=True) + jnp.exp2(sink_rows[j] - m)
        vb = vt_scr[HEAD_DIM * j:HEAD_DIM * (j + 1), r0:r0 + 2 * ATT_BLOCK]
        ot = _dot(vb, p.astype(BF16)) * (1.0 / denom)
        for g in range(GQA_GROUP):
            hh = GQA_GROUP * j + g
            oatt_scr[HEAD_DIM * hh:HEAD_DIM * (hh + 1), r0:r0 + ATT_BLOCK] = (
                ot[:, g * ATT_BLOCK:(g + 1) * ATT_BLOCK].astype(BF16))

    att_units = [(c, j) for j in range(N_KV_HEADS) for c in range(tm // ATT_BLOCK)]
    att_sched = spread(att_items, len(att_units))
    st_next = att_scores(*att_units[0])
    for u, unit in enumerate(att_units):
        st_cur = st_next
        if u + 1 < len(att_units):
            st_next = att_scores(*att_units[u + 1])
        for item in att_sched[u]:
            item()
        att_finish(*unit, st_cur)

    kband_scr[0:ATT_BLOCK, :] = kband_scr[tm:tm + ATT_BLOCK, :]
    vt_scr[:, 0:ATT_BLOCK] = vt_scr[:, tm:tm + ATT_BLOCK]

    def ret_scores(c, hd):
        r0 = c * RET_BLOCK
        q = rq_scr[RET_QK_DIM * hd:RET_QK_DIM * (hd + 1), r0:r0 + RET_BLOCK]
        kk = rk_scr[r0:r0 + RET_BLOCK, RET_QK_DIM * hd:RET_QK_DIM * (hd + 1)]
        return (_dot(kk, q) * dmask_ref[hd]).astype(BF16)

    def ret_finish(c, hd, s):
        r0 = c * RET_BLOCK
        qx = rqx_scr[RET_QK_DIM * hd:RET_QK_DIM * (hd + 1), r0:r0 + RET_BLOCK]
        kk = rk_scr[r0:r0 + RET_BLOCK, RET_QK_DIM * hd:RET_QK_DIM * (hd + 1)]
        vv = rv_scr[RET_V_DIM * hd:RET_V_DIM * (hd + 1), r0:r0 + RET_BLOCK]
        vz = rvz_scr[RET_V_DIM * hd:RET_V_DIM * (hd + 1), r0:r0 + RET_BLOCK]
        state = state_scr[hd]
        out = _dot(jnp.concatenate([vv, state.astype(BF16)], axis=1),
                   jnp.concatenate([s, qx], axis=0))
        kv = _dot(vz, kk)
        state_scr[hd] = state * gch_ref[hd] + kv
        rn = out * lax.rsqrt(jnp.mean(out * out, axis=0, keepdims=True) + EPS)
        rb_scr[RET_V_DIM * hd:RET_V_DIM * (hd + 1), r0:r0 + RET_BLOCK] = (
            rn * silu_scr[RET_V_DIM * hd:RET_V_DIM * (hd + 1), r0:r0 + RET_BLOCK]).astype(BF16)

    ret_units = [(c, hd) for c in range(tm // RET_BLOCK) for hd in range(RET_HEADS)]
    ret_sched = spread(ret_items, len(ret_units))
    s_next = ret_scores(*ret_units[0])
    for u, unit in enumerate(ret_units):
        s_cur = s_next
        if u + 1 < len(ret_units):
            s_next = ret_scores(*ret_units[u + 1])
        for item in ret_sched[u]:
            item()
        ret_finish(*unit, s_cur)

    norm_to_scratch(xnext_ref[0])
    rb = rb_scr[...]
    for i in range(nblk):
        rows = slice(i * ROW_BLOCK, (i + 1) * ROW_BLOCK)
        yb = _dot(wbt_ref[rows, :], rb)
        mixedb_scr[rows, :] = (mixed_scr[rows, :] + sgb_scr[rows, :] * yb).astype(BF16)
    o_ref[0] = x_ref[0] + _dot_tn(mixedb_scr[...], wout_ref[...])


def _mlp_kernel(x_ref, g_ref, wup_ref, wdown_ref, gfin_ref, o_ref, a_scr, *, final_norm):
    x = x_ref[...]
    h = _rmsnorm(x, g_ref[...]).astype(BF16)
    for j in range(D_FF // FF_CHUNK):
        u = _dot(h, wup_ref[:, j * FF_CHUNK:(j + 1) * FF_CHUNK])
        a_scr[:, j * FF_CHUNK:(j + 1) * FF_CHUNK] = jnp.square(jnp.maximum(u, 0.0)).astype(BF16)
    y = x + _dot(a_scr[...], wdown_ref[...])
    if final_norm:
        y = _rmsnorm(y, gfin_ref[...])
    o_ref[...] = y


def _resident(shape):
    return pl.BlockSpec(shape, lambda *_: (0,) * len(shape), pipeline_mode=pl.Buffered(1))


def _resident_layer(stacked, layer):
    tail = stacked.shape[1:]
    return pl.BlockSpec((None,) + tail, lambda *_: (layer,) + (0,) * len(tail),
                        pipeline_mode=pl.Buffered(1))


def _mixer_call(x, layer, g, wt, wrq, wk, wat, wbt, wout, sinks, gch, tables, w_up, w_down):
    bsz, seq, d = x.shape
    tm = min(SEQ_TILE, seq)
    feat_tab, tok_tab, band, dmask, zeta, xi = tables
    tok_spec = pl.BlockSpec((tm, tok_tab.shape[1]), lambda b, t: (t, 0))
    feat_spec = pl.BlockSpec((feat_tab.shape[0], tm), lambda b, t: (0, t))
    smem = pl.BlockSpec(memory_space=pltpu.SMEM)
    x_spec = pl.BlockSpec((1, tm, d), lambda b, t: (b, t, 0))
    nt = seq // tm

    def next_tile(b, t):
        flat = jnp.minimum(b * nt + t + 1, bsz * nt - 1)
        return (flat // nt, flat % nt, 0)

    steps = bsz * nt
    n_cast = max(nb for nb in (1, 2, 4, 8, 16, 32, 64) if steps % nb == 0 and nb <= CAST_BLOCKS)
    cast_every = steps // n_cast

    def cast_specs(w):
        rows, cols = w.shape[1] // n_cast, w.shape[2]
        blk = lambda b, t: ((b * nt + t) // cast_every, 0)
        return (pl.BlockSpec((None, rows, cols), lambda b, t: (layer,) + blk(b, t)),
                pl.BlockSpec((rows, cols), blk))

    (wup_in, wup_out), (wdown_in, wdown_out) = cast_specs(w_up), cast_specs(w_down)
    return pl.pallas_call(
        functools.partial(_mixer_kernel, cast_every=cast_every),
        grid=(bsz, nt),
        in_specs=[x_spec, pl.BlockSpec((1, tm, d), next_tile)]
        + [_resident_layer(w, layer) for w in (g, wt, wrq, wk, wat, wbt, wout)]
        + [smem, smem, feat_spec, tok_spec,
           _resident(band.shape), _resident(dmask.shape), _resident(zeta.shape), _resident(xi.shape),
           wup_in, wdown_in],
        out_specs=(x_spec, wup_out, wdown_out),
        out_shape=(jax.ShapeDtypeStruct(x.shape, F32),
                   jax.ShapeDtypeStruct(w_up.shape[1:], BF16),
                   jax.ShapeDtypeStruct(w_down.shape[1:], BF16)),
        scratch_shapes=[
            pltpu.VMEM((tm, d), BF16),
            pltpu.VMEM((d, tm), BF16),
            pltpu.VMEM((N_KV_HEADS, LANES, GQA_GROUP * tm), BF16),
            pltpu.VMEM((tm + ATT_BLOCK, LANES), BF16),
            pltpu.VMEM((ATT_KV, tm + ATT_BLOCK), BF16),
            pltpu.VMEM((ATT_Q, tm), BF16),
            pltpu.VMEM((RET_QK, tm), BF16),
            pltpu.VMEM((RET_QK, tm), BF16),
            pltpu.VMEM((tm, RET_QK), BF16),
            pltpu.VMEM((RET_V, tm), BF16),
            pltpu.VMEM((RET_V, tm), BF16),
            pltpu.VMEM((RET_V, tm), F32),
            pltpu.VMEM((RET_V, tm), BF16),
            pltpu.VMEM((RET_HEADS, RET_V_DIM, RET_QK_DIM), F32),
            pltpu.VMEM((d, tm), F32),
            pltpu.VMEM((d, tm), F32),
            pltpu.VMEM((d, tm), BF16),
        ],
        compiler_params=pltpu.CompilerParams(
            dimension_semantics=("arbitrary", "arbitrary"),
            vmem_limit_bytes=VMEM_LIMIT_BYTES),
        name="mixer",
    )(x, x, g, wt, wrq, wk, wat, wbt, wout, sinks, gch, feat_tab, tok_tab, band, dmask, zeta, xi,
      w_up, w_down)


def _mlp_call(x2d, layer, g, wup, wdown, gfin, final_norm):
    n, d = x2d.shape
    tm = min(MLP_TILE, n)
    x_spec = pl.BlockSpec((tm, d), lambda i: (i, 0))
    return pl.pallas_call(
        functools.partial(_mlp_kernel, final_norm=final_norm),
        grid=(n // tm,),
        in_specs=[x_spec, _resident_layer(g, layer), _resident(wup.shape), _resident(wdown.shape),
                  _resident((1, d))],
        out_specs=x_spec,
        out_shape=jax.ShapeDtypeStruct(x2d.shape, F32),
        scratch_shapes=[pltpu.VMEM((tm, D_FF), BF16)],
        compiler_params=pltpu.CompilerParams(
            dimension_semantics=("arbitrary",),
            vmem_limit_bytes=VMEM_LIMIT_BYTES),
        name="mlp_final" if final_norm else "mlp",
    )(x2d, g, wup, wdown, gfin)


def _even_odd_heads(w):
    lead = w.shape[:-1]
    w = w.reshape(lead + (RET_HEADS, RET_QK_DIM // 2, 2))
    return jnp.swapaxes(w, -1, -2).reshape(lead + (RET_QK,))


def _halves_first(w):
    lead = w.shape[:-1]
    w = w.reshape(lead + (N_KV_HEADS, 2, HALF))
    return jnp.swapaxes(w, -2, -3).reshape(lead + (ATT_KV,))


def _prepare_in_projection(w_in):
    wt = jnp.swapaxes(w_in, 1, 2).astype(BF16)
    wrq = jnp.swapaxes(_even_odd_heads(w_in[:, :, OFF_RQ:OFF_RK]), 1, 2).astype(BF16)
    wk = jnp.concatenate([_halves_first(w_in[:, :, OFF_AK:OFF_AV]),
                          _even_odd_heads(w_in[:, :, OFF_RK:OFF_RV])], axis=2).astype(BF16)
    return wt, wrq, wk


def _as_f32(a):
    return np.ascontiguousarray(a, dtype=np.float32)


@functools.lru_cache(maxsize=None)
def _position_tables(seq):
    pos = np.arange(seq, dtype=np.float64)
    inv = ROPE_THETA ** (-np.arange(HALF, dtype=np.float64) / HALF)
    ang = pos[:, None] * inv[None, :]
    cos, sin = np.cos(ang), np.sin(ang)
    theta = 1.0 / (RET_THETA ** np.linspace(0.0, 1.0, RET_QK_DIM // 2))
    angr = pos[:, None] * theta[None, :]
    cosr, sinr = np.cos(angr), np.sin(angr)
    qs = HEAD_DIM ** -0.5 * LOG2E
    ks = RET_QK_DIM ** -0.5
    tok_tab = np.concatenate([cos, cos, cos, cos, -sin, -sin, sin, sin,
                              ks * cosr, ks * cosr, -ks * sinr, ks * sinr], axis=1)
    feat_tab = np.concatenate([qs * cos, qs * sin, cosr, sinr], axis=1).T

    kj = np.arange(2 * ATT_BLOCK)[:, None]
    qi = np.arange(GQA_GROUP * ATT_BLOCK)[None, :] % ATT_BLOCK
    inside = (kj > qi) & (kj <= qi + WINDOW)
    band = np.where(np.stack([inside, inside & (kj >= ATT_BLOCK)]), 0.0, -np.inf)

    log_g = np.log(1.0 - 2.0 ** (-5.0 - np.arange(RET_HEADS, dtype=np.float64)))
    idx = np.arange(RET_BLOCK, dtype=np.float64)
    rel = idx[None, :] - idx[:, None]
    dmask = np.where(rel[None] >= 0, np.exp(log_g[:, None, None] * np.maximum(rel, 0.0)[None]), 0.0)
    zeta = np.exp(log_g[:, None] * (RET_BLOCK - 1.0 - idx)[None, :])
    xi = np.exp(log_g[:, None] * (idx + 1.0)[None, :])
    zeta = np.broadcast_to(zeta[:, None, :], (RET_HEADS, 8, RET_BLOCK))
    xi = np.broadcast_to(xi[:, None, :], (RET_HEADS, 8, RET_BLOCK))
    gch = np.exp(log_g * RET_BLOCK)
    return tuple(_as_f32(a) for a in (feat_tab, tok_tab, band, dmask, zeta, xi)), _as_f32(gch)


def kernel(x, g_mix, w_in, sinks, w_a, w_b, w_out, g_mlp, w_up, w_down, g_final):
    bsz, seq, d = x.shape
    depth = w_in.shape[0]
    tables, gch = _position_tables(seq)
    wt, wrq, wk = _prepare_in_projection(w_in)
    wat = jnp.swapaxes(w_a, 1, 2).astype(BF16)
    wbt = jnp.swapaxes(w_b, 1, 2).astype(BF16)
    wout = w_out.astype(BF16)
    g_mix3 = g_mix[:, None, :]
    g_mlp3 = g_mlp[:, None, :]
    for l in range(depth):
        x, wup, wdown = _mixer_call(x, l, g_mix3, wt, wrq, wk, wat, wbt, wout, sinks[l], gch, tables,
                                    w_up, w_down)
        x = _mlp_call(x.reshape(bsz * seq, d), l, g_mlp3, wup, wdown, g_final[None, :],
                      final_norm=(l == depth - 1)).reshape(bsz, seq, d)
    return x
```
